```python
import jax, jax.numpy as jnp
from jax import lax
import numpy as np

D_MODEL = 1024
BATCH = 8
SEQ = 4096
DEPTH = 2

GLA_HEADS = 4
GLA_DK = 128
GLA_DV = 128
GLA_RANK = 16
GLA_TAU = 16.0
GLA_CHUNK = 64
CONV_CH = D_MODEL // 2
CONV_K = 3
SWA_Q_HEADS = 8
SWA_KV_HEADS = 2
SWA_HEAD_DIM = 64
SWA_WINDOW = 128
D_FF = 2816
FFN_CONV_K = 3
N_BRANCH = 3
EPS = 1e-6

MIX_SIZES = (
    GLA_HEADS * GLA_DK,
    GLA_HEADS * GLA_DK,
    GLA_HEADS * GLA_DV,
    GLA_HEADS * GLA_DV,
    GLA_RANK,
    CONV_CH,
    CONV_CH,
    CONV_CH,
    SWA_Q_HEADS * SWA_HEAD_DIM,
    SWA_KV_HEADS * SWA_HEAD_DIM,
    SWA_KV_HEADS * SWA_HEAD_DIM,
    N_BRANCH * D_MODEL,
)
D_IN_TOTAL = sum(MIX_SIZES)

kernel_name = 'hybrid_gla_shortconv_swa_sink_block'


def rmsnorm(x, g):
    xf = x.astype(jnp.float32)
    y = xf * lax.rsqrt(jnp.mean(xf * xf, axis=-1, keepdims=True) + EPS)
    return (y * g.astype(jnp.float32)).astype(x.dtype)


def split_columns(z):
    out = []
    off = 0
    for n in MIX_SIZES:
        out.append(z[..., off:off + n])
        off += n
    return out


def causal_dwconv(u, w):
    K = w.shape[0]
    S = u.shape[1]
    up = jnp.pad(u, ((0, 0), (K - 1, 0), (0, 0)))
    y = up[:, 0:S] * w[0]
    for i in range(1, K):
        y = y + up[:, i:i + S] * w[i]
    return y


def alibi_slopes(n_heads):
    return jnp.exp2(-(8.0 / n_heads) * jnp.arange(1, n_heads + 1, dtype=jnp.float32))


def gla_chunked(q, k, v, log_a):
    f32 = jnp.float32
    Bsz, S, H, dk = q.shape
    dv = v.shape[-1]
    C = GLA_CHUNK
    N = S // C
    q = (q.astype(f32) * dk ** -0.5).reshape(Bsz, N, C, H, dk)
    k = k.astype(f32).reshape(Bsz, N, C, H, dk)
    v = v.astype(f32).reshape(Bsz, N, C, H, dv)
    b = jnp.cumsum(log_a.astype(f32).reshape(Bsz, N, C, H, dk), axis=2)
    b_ref = b[:, :, C // 2 - 1:C // 2]
    b_last = b[:, :, C - 1:]
    causal = jnp.tril(jnp.ones((C, C), dtype=bool))
    attn = jnp.einsum('bnihk,bnjhk->bnhij', q * jnp.exp(b - b_ref), k * jnp.exp(b_ref - b))
    attn = jnp.where(causal, attn, 0.0)
    o_intra = jnp.einsum('bnhij,bnjhv->bnihv', attn, v)
    kv = jnp.einsum('bnjhk,bnjhv->bnhkv', k * jnp.exp(b_last - b), v)
    decay = jnp.exp(b_last[:, :, 0])

    def step(state, inp):
        d, kv_n = inp
        return d[..., None] * state + kv_n, state

    s0 = jnp.zeros((Bsz, H, dk, dv), f32)
    _, s_prev = lax.scan(step, s0, (jnp.moveaxis(decay, 1, 0), jnp.moveaxis(kv, 1, 0)))
    s_prev = jnp.moveaxis(s_prev, 0, 1)
    o_inter = jnp.einsum('bnihk,bnhkv->bnihv', q * jnp.exp(b), s_prev)
    return (o_intra + o_inter).reshape(Bsz, S, H, dv)


def swa_sink_attention(q, k, v, sinks, slopes):
    f32 = jnp.float32
    Bsz, S, Hq, hd = q.shape
    Hkv = k.shape[2]
    G = Hq // Hkv
    W = SWA_WINDOW
    Nb = S // W
    qb = (q.astype(f32) * hd ** -0.5).reshape(Bsz, Nb, W, Hkv, G, hd)

    def windows(t):
        tp = jnp.pad(t.astype(f32), ((0, 0), (W, 0), (0, 0), (0, 0))).reshape(Bsz, Nb + 1, W, Hkv, hd)
        return jnp.concatenate([tp[:, :-1], tp[:, 1:]], axis=2)

    kw = windows(k)
    vw = windows(v)
    scores = jnp.einsum('bnqhgd,bnkhd->bnhgqk', qb, kw)
    iq = jnp.arange(W)[:, None]
    jk = jnp.arange(2 * W)[None, :]
    dist = W + iq - jk
    blk = jnp.arange(Nb)[:, None, None]
    valid = (dist >= 0) & (dist < W) & (blk * W - W + jk >= 0)
    alibi = -slopes.reshape(Hkv, G)[:, :, None, None] * dist.astype(f32)
    logits = jnp.where(valid[None, :, None, None], scores + alibi, -jnp.inf)
    sink = sinks.astype(f32).reshape(Hkv, G)[None, None, :, :, None]
    m = jnp.maximum(logits.max(axis=-1), sink)
    p = jnp.exp(logits - m[..., None])
    denom = p.sum(axis=-1) + jnp.exp(sink - m)
    o = jnp.einsum('bnhgqk,bnkhd->bnqhgd', p, vw) / jnp.moveaxis(denom, 4, 2)[..., None]
    return o.reshape(Bsz, S, Hq * hd)


def setup_inputs(seed: int = 0) -> dict:
    key = jax.random.key(seed)
    ks = jax.random.split(key, 17)
    f32 = jnp.float32

    def nrm(k, shape, scale):
        return jax.random.normal(k, shape, f32) * scale

    return {
        'x': nrm(ks[0], (BATCH, SEQ, D_MODEL), 1.0),
        'g_mix': 1.0 + nrm(ks[1], (DEPTH, D_MODEL), 0.02),
        'w_in': nrm(ks[2], (DEPTH, D_MODEL, D_IN_TOTAL), D_MODEL ** -0.5),
        'gla_w_alpha': nrm(ks[3], (DEPTH, GLA_RANK, GLA_HEADS * GLA_DK), GLA_RANK ** -0.5),
        'gla_b_alpha': nrm(ks[4], (DEPTH, GLA_HEADS * GLA_DK), 0.02),
        'gla_norm_g': 1.0 + nrm(ks[5], (DEPTH, GLA_HEADS * GLA_DV), 0.02),
        'conv_w': nrm(ks[6], (DEPTH, CONV_K, CONV_CH), CONV_K ** -0.5),
        'swa_sinks': nrm(ks[7], (DEPTH, SWA_Q_HEADS), 0.5),
        'w_gla_o': nrm(ks[8], (DEPTH, GLA_HEADS * GLA_DV, D_MODEL), (GLA_HEADS * GLA_DV) ** -0.5),
        'w_conv_o': nrm(ks[9], (DEPTH, CONV_CH, D_MODEL), CONV_CH ** -0.5),
        'w_swa_o': nrm(ks[10], (DEPTH, SWA_Q_HEADS * SWA_HEAD_DIM, D_MODEL), (SWA_Q_HEADS * SWA_HEAD_DIM) ** -0.5),
        'w_o': nrm(ks[11], (DEPTH, D_MODEL, D_MODEL), D_MODEL ** -0.5),
        'g_ffn': 1.0 + nrm(ks[12], (DEPTH, D_MODEL), 0.02),
        'w_up': nrm(ks[13], (DEPTH, D_MODEL, 2 * D_FF), D_MODEL ** -0.5),
        'ffn_conv_w': nrm(ks[14], (DEPTH, FFN_CONV_K, 2 * D_FF), FFN_CONV_K ** -0.5),
        'w_down': nrm(ks[15], (DEPTH, D_FF, D_MODEL), D_FF ** -0.5),
        'g_final': 1.0 + nrm(ks[16], (D_MODEL,), 0.02),
    }


def reference(x, g_mix, w_in, gla_w_alpha, gla_b_alpha, gla_norm_g, conv_w, swa_sinks, w_gla_o, w_conv_o, w_swa_o, w_o, g_ffn, w_up, ffn_conv_w, w_down, g_final):
    f32 = jnp.float32
    Bsz, S, _ = x.shape
    slopes = alibi_slopes(SWA_Q_HEADS)
    h = x
    for l in range(DEPTH):
        u = rmsnorm(h, g_mix[l])
        z = u @ w_in[l]
        gq, gk, gv, gr, ga, cx, cb, cc, sq, sk, sv, gates = split_columns(z)

        log_a = jax.nn.log_sigmoid((ga @ gla_w_alpha[l] + gla_b_alpha[l]).astype(f32)) / GLA_TAU
        o = gla_chunked(gq.reshape(Bsz, S, GLA_HEADS, GLA_DK),
                        gk.reshape(Bsz, S, GLA_HEADS, GLA_DK),
                        gv.reshape(Bsz, S, GLA_HEADS, GLA_DV),
                        log_a.reshape(Bsz, S, GLA_HEADS, GLA_DK))
        o = o * lax.rsqrt(jnp.mean(o * o, axis=-1, keepdims=True) + EPS)
        o = o.reshape(Bsz, S, GLA_HEADS * GLA_DV) * gla_norm_g[l].astype(f32)
        y_gla = (o * jax.nn.silu(gr.astype(f32))).astype(x.dtype) @ w_gla_o[l]

        y_conv = (cb * causal_dwconv(cc * cx, conv_w[l])) @ w_conv_o[l]

        o = swa_sink_attention(sq.reshape(Bsz, S, SWA_Q_HEADS, SWA_HEAD_DIM),
                               sk.reshape(Bsz, S, SWA_KV_HEADS, SWA_HEAD_DIM),
                               sv.reshape(Bsz, S, SWA_KV_HEADS, SWA_HEAD_DIM),
                               swa_sinks[l], slopes)
        y_swa = o.astype(x.dtype) @ w_swa_o[l]

        gt = jax.nn.sigmoid(gates)
        merged = (gt[..., :D_MODEL] * y_gla
                  + gt[..., D_MODEL:2 * D_MODEL] * y_conv
                  + gt[..., 2 * D_MODEL:] * y_swa)
        h = h + merged @ w_o[l]

        u = rmsnorm(h, g_ffn[l])
        hid = causal_dwconv(u @ w_up[l], ffn_conv_w[l])
        h = h + (jax.nn.silu(hid[..., :D_FF]) * hid[..., D_FF:]) @ w_down[l]
    return rmsnorm(h, g_final)
```

```python
import functools

import jax
import jax.numpy as jnp
from jax import lax
from jax.experimental import pallas as pl
from jax.experimental.pallas import tpu as pltpu

F32 = jnp.float32
BF16 = jnp.bfloat16

D_MODEL = 1024
GLA_HEADS = 4
GLA_DK = 128
GLA_DV = 128
GLA_RANK = 16
GLA_TAU = 16.0
GLA_CHUNK = 64
CONV_CH = D_MODEL // 2
SWA_Q_HEADS = 8
SWA_KV_HEADS = 2
SWA_GROUP = SWA_Q_HEADS // SWA_KV_HEADS
SWA_HEAD_DIM = 64
SWA_WINDOW = 128
D_FF = 2816
EPS = 1e-6

GLA_W = GLA_HEADS * GLA_DK
SWA_QW = SWA_Q_HEADS * SWA_HEAD_DIM
SWA_KW = SWA_KV_HEADS * SWA_HEAD_DIM

_SRC_GLA = 0
_SRC_GA = 4 * GLA_W
_SRC_CONV = _SRC_GA + GLA_RANK
_SRC_SWA = _SRC_CONV + 3 * CONV_CH
_SRC_GATES = _SRC_SWA + SWA_QW + 2 * SWA_KW
_SRC_END = _SRC_GATES + 3 * D_MODEL

Z_Q, Z_K, Z_V, Z_R = 0, GLA_W, 2 * GLA_W, 3 * GLA_W
Z_GATES = 4 * GLA_W
Z_CX = Z_GATES + 3 * D_MODEL
Z_CB = Z_CX + CONV_CH
Z_CC = Z_CB + CONV_CH
Z_SQ = Z_CC + CONV_CH
Z_SK = Z_SQ + SWA_QW
Z_SV = Z_SK + SWA_KW
NZ = Z_SV + SWA_KW

LANE = 128
SUBLANE = 8
TOKEN_TILE = 512
MIX_TILE = 512
FF_TILE = 256
PROJ_CHUNK = 512
VMEM_LIMIT = 56 * 1024 * 1024


def _rms(x, g):
    return x * lax.rsqrt(jnp.mean(x * x, axis=-1, keepdims=True) + EPS) * g


def _dot(a, b):
    return jnp.dot(a, b, preferred_element_type=F32)


def _dot_nt(a, b):
    return lax.dot_general(a, b, (((1,), (1,)), ((), ())), preferred_element_type=F32)


def _dot_tn(a, b):
    return lax.dot_general(a, b, (((0,), (0,)), ((), ())), preferred_element_type=F32)


def _sigmoid(x):
    return 1.0 / (1.0 + jnp.exp(-x))


def _split3(x):
    hi = x.astype(BF16)
    r1 = x - hi.astype(F32)
    mid = r1.astype(BF16)
    lo = (r1 - mid.astype(F32)).astype(BF16)
    return hi, mid, lo


def _resident(shape):
    nd = len(shape)
    return pl.BlockSpec(shape, lambda *_: (0,) * nd, pipeline_mode=pl.Buffered(1))


def _params(n_axes):
    return pltpu.CompilerParams(
        dimension_semantics=("arbitrary",) * n_axes, vmem_limit_bytes=VMEM_LIMIT)


def _inproj_kernel(h_ref, g_ref, w_ref, wga_ref, wa_hi_ref, wa_lo_ref, ba_ref, z_ref, la_ref):
    u = _rms(h_ref[...], g_ref[...]).astype(BF16)
    for off in range(0, NZ, PROJ_CHUNK):
        wd = min(PROJ_CHUNK, NZ - off)
        z_ref[:, off:off + wd] = _dot(u, w_ref[:, off:off + wd]).astype(BF16)
    ga = _dot(u, wga_ref[...])
    ga_hi = ga.astype(BF16)
    ga_lo = (ga - ga_hi.astype(F32)).astype(BF16)
    pre = (_dot(ga_hi, wa_hi_ref[...]) + _dot(ga_lo, wa_hi_ref[...])
           + _dot(ga_hi, wa_lo_ref[...]) + ba_ref[...])
    log_sig = jnp.minimum(pre, 0.0) - jnp.log1p(jnp.exp(-jnp.abs(pre)))
    la_ref[...] = log_sig * (1.0 / GLA_TAU)


def _inproj(h, g, w_main, w_ga, wa_hi, wa_lo, b_alpha):
    t = h.shape[0]
    tm = TOKEN_TILE
    return pl.pallas_call(
        _inproj_kernel,
        grid=(t // tm,),
        in_specs=[
            pl.BlockSpec((tm, D_MODEL), lambda i: (i, 0)),
            _resident((1, D_MODEL)),
            _resident((D_MODEL, NZ)),
            _resident((D_MODEL, LANE)),
            _resident((LANE, GLA_W)),
            _resident((LANE, GLA_W)),
            _resident((1, GLA_W)),
        ],
        out_specs=[
            pl.BlockSpec((tm, NZ), lambda i: (i, 0)),
            pl.BlockSpec((tm, GLA_W), lambda i: (i, 0)),
        ],
        out_shape=[
            jax.ShapeDtypeStruct((t, NZ), BF16),
            jax.ShapeDtypeStruct((t, GLA_W), F32),
        ],
        compiler_params=_params(1),
        name="inproj",
    )(h, g, w_main, w_ga, wa_hi, wa_lo, b_alpha)


def _gla_kernel(tiles_per_seq, q_ref, k_ref, v_ref, r_ref, la_ref, g_ref, o_ref, st_ref):
    c_len = GLA_CHUNK

    @pl.when(pl.program_id(0) % tiles_per_seq == 0)
    def _():
        st_ref[...] = jnp.zeros_like(st_ref)

    row = lax.broadcasted_iota(jnp.int32, (c_len, c_len), 0)
    col = lax.broadcasted_iota(jnp.int32, (c_len, c_len), 1)
    causal = row >= col
    tril = jnp.where(causal, 1.0, 0.0).astype(BF16)
    scale = GLA_DK ** -0.5
    gain = g_ref[...]

    def chunk(c, carry):
        r0 = pl.multiple_of(c * c_len, c_len)
        rows = pl.ds(r0, c_len)
        la_hi, la_mid, la_lo = _split3(la_ref[rows, :])
        b = _dot(tril, la_hi) + _dot(tril, la_mid) + _dot(tril, la_lo)
        b_mid = b[c_len // 2 - 1:c_len // 2, :]
        b_last = b[c_len - 1:c_len, :]
        q = q_ref[rows, :].astype(F32) * scale
        k = k_ref[rows, :].astype(F32)
        v = v_ref[rows, :]
        q_in = (q * jnp.exp(b - b_mid)).astype(BF16)
        k_in = (k * jnp.exp(b_mid - b)).astype(BF16)
        q_ex = (q * jnp.exp(b)).astype(BF16)
        k_ex = (k * jnp.exp(b_last - b)).astype(BF16)
        decay = jnp.exp(b_last)
        for h in range(GLA_HEADS):
            hs = slice(h * GLA_DK, (h + 1) * GLA_DK)
            attn = jnp.where(causal, _dot_nt(q_in[:, hs], k_in[:, hs]), 0.0).astype(BF16)
            st = st_ref[h]
            o = _dot(attn, v[:, hs]) + _dot_nt(q_ex[:, hs], st.astype(BF16))
            st_ref[h] = st * decay[:, hs] + _dot_tn(v[:, hs], k_ex[:, hs])
            o = o * lax.rsqrt(jnp.mean(o * o, axis=-1, keepdims=True) + EPS) * gain[:, hs]
            r = r_ref[rows, hs].astype(F32)
            o_ref[rows, hs] = (o * (r * _sigmoid(r))).astype(BF16)
        return carry

    lax.fori_loop(0, o_ref.shape[0] // c_len, chunk, 0)


def _gla(z, la, gain, seq_len):
    t = z.shape[0]
    tl = MIX_TILE
    blk = lambda j: pl.BlockSpec((tl, GLA_W), lambda i: (i, j))
    return pl.pallas_call(
        functools.partial(_gla_kernel, seq_len // tl),
        grid=(t // tl,),
        in_specs=[blk(Z_Q // GLA_W), blk(Z_K // GLA_W), blk(Z_V // GLA_W), blk(Z_R // GLA_W),
                  pl.BlockSpec((tl, GLA_W), lambda i: (i, 0)),
                  _resident((1, GLA_W))],
        out_specs=pl.BlockSpec((tl, GLA_W), lambda i: (i, 0)),
        out_shape=jax.ShapeDtypeStruct((t, GLA_W), BF16),
        scratch_shapes=[pltpu.VMEM((GLA_HEADS, GLA_DV, GLA_DK), F32)],
        compiler_params=_params(1),
        name="gla",
    )(z, z, z, z, la, gain)


def _swa_kernel(tiles_per_seq, sink_ref, q_ref, k_ref, v_ref, kp_ref, vp_ref, o_ref):
    w = SWA_WINDOW
    hd = SWA_HEAD_DIM
    grp = SWA_GROUP
    n_blk = q_ref.shape[0] // w
    seq_start = pl.program_id(0) % tiles_per_seq == 0
    scale = hd ** -0.5

    row = lax.broadcasted_iota(jnp.int32, (grp * w, 2 * w), 0)
    col = lax.broadcasted_iota(jnp.int32, (grp * w, 2 * w), 1)
    head_in_group = row // w
    dist = w + (row % w) - col
    valid = jnp.logical_and(dist >= 0, dist < w)
    valid_first = jnp.logical_and(valid, jnp.logical_or(col >= w, jnp.logical_not(seq_start)))
    grp_col = lax.broadcasted_iota(jnp.int32, (grp * w, 1), 0) // w

    for h in range(SWA_KV_HEADS):
        slope = jnp.zeros((grp * w, 2 * w), F32)
        sink = jnp.zeros((grp * w, 1), F32)
        for g in range(grp):
            hq = h * grp + g
            slope = jnp.where(head_in_group == g, 2.0 ** (-(8.0 / SWA_Q_HEADS) * (hq + 1)), slope)
            sink = jnp.where(grp_col == g, sink_ref[hq], sink)
        alibi = -slope * dist.astype(F32)
        ks = slice(h * hd, (h + 1) * hd)
        keys = jnp.concatenate([kp_ref[:, ks], k_ref[:, ks]], axis=0)
        vals = jnp.concatenate([vp_ref[:, ks], v_ref[:, ks]], axis=0)
        for j in range(n_blk):
            qs = [q_ref[j * w:(j + 1) * w, (h * grp + g) * hd:(h * grp + g + 1) * hd]
                  for g in range(grp)]
            q = (jnp.concatenate(qs, axis=0).astype(F32) * scale).astype(BF16)
            s = _dot_nt(q, keys[j * w:(j + 2) * w, :]) + alibi
            logits = jnp.where(valid_first if j == 0 else valid, s, -jnp.inf)
            m = jnp.maximum(jnp.max(logits, axis=-1, keepdims=True), sink)
            p = jnp.exp(logits - m)
            denom = jnp.sum(p, axis=-1, keepdims=True) + jnp.exp(sink - m)
            o = _dot(p.astype(BF16), vals[j * w:(j + 2) * w, :]) / denom
            for g in range(grp):
                hq = h * grp + g
                o_ref[j * w:(j + 1) * w, hq * hd:(hq + 1) * hd] = o[g * w:(g + 1) * w, :].astype(BF16)


def _swa(z, sinks, seq_len):
    t = z.shape[0]
    tl = MIX_TILE
    w = SWA_WINDOW
    prev = lambda j: pl.BlockSpec((w, SWA_KW), lambda i: (jnp.maximum(i * (tl // w) - 1, 0), j))
    return pl.pallas_call(
        functools.partial(_swa_kernel, seq_len // tl),
        grid=(t // tl,),
        in_specs=[
            pl.BlockSpec(memory_space=pltpu.SMEM),
            pl.BlockSpec((tl, SWA_QW), lambda i: (i, Z_SQ // SWA_QW)),
            pl.BlockSpec((tl, SWA_KW), lambda i: (i, Z_SK // SWA_KW)),
            pl.BlockSpec((tl, SWA_KW), lambda i: (i, Z_SV // SWA_KW)),
            prev(Z_SK // SWA_KW),
            prev(Z_SV // SWA_KW),
        ],
        out_specs=pl.BlockSpec((tl, SWA_QW), lambda i: (i, 0)),
        out_shape=jax.ShapeDtypeStruct((t, SWA_QW), BF16),
        compiler_params=_params(1),
        name="swa",
    )(sinks, z, z, z, z, z)


def _causal_conv3(y, prev, w):
    row = lax.broadcasted_iota(jnp.int32, y.shape, 0)
    p1 = prev[SUBLANE - 1:SUBLANE, :]
    p2 = prev[SUBLANE - 2:SUBLANE - 1, :]
    y1 = jnp.where(row == 0, p1, pltpu.roll(y, 1, 0))
    y2 = jnp.where(row == 0, p2, jnp.where(row == 1, p1, pltpu.roll(y, 2, 0)))
    return w[0:1, :] * y2 + w[1:2, :] * y1 + w[2:3, :] * y


def _merge_kernel(tiles_per_seq, h_ref, og_ref, os_ref, cx_ref, cb_ref, cc_ref, cxp_ref, ccp_ref,
                  g1_ref, g2_ref, g3_ref, cw_ref, wg_ref, wc_ref, ws_ref, wo_ref, o_ref):
    seq_start = pl.program_id(0) % tiles_per_seq == 0
    prod = cc_ref[...].astype(F32) * cx_ref[...].astype(F32)
    prev = cxp_ref[...].astype(F32) * ccp_ref[...].astype(F32)
    prev = jnp.where(seq_start, 0.0, prev)
    conv = cb_ref[...].astype(F32) * _causal_conv3(prod, prev, cw_ref[...])
    y_gla = _dot(og_ref[...], wg_ref[...])
    y_conv = _dot(conv.astype(BF16), wc_ref[...])
    y_swa = _dot(os_ref[...], ws_ref[...])
    merged = (_sigmoid(g1_ref[...].astype(F32)) * y_gla
              + _sigmoid(g2_ref[...].astype(F32)) * y_conv
              + _sigmoid(g3_ref[...].astype(F32)) * y_swa)
    o_ref[...] = h_ref[...] + _dot(merged.astype(BF16), wo_ref[...])


def _merge(h, o_gla, o_swa, z, conv_w, w_gla_o, w_conv_o, w_swa_o, w_o, seq_len):
    t = h.shape[0]
    tm = TOKEN_TILE
    zblk = lambda width, off: pl.BlockSpec((tm, width), lambda i: (i, off // width))
    halo = lambda off: pl.BlockSpec(
        (SUBLANE, CONV_CH), lambda i: (jnp.maximum(i * (tm // SUBLANE) - 1, 0), off // CONV_CH))
    return pl.pallas_call(
        functools.partial(_merge_kernel, seq_len // tm),
        grid=(t // tm,),
        in_specs=[
            pl.BlockSpec((tm, D_MODEL), lambda i: (i, 0)),
            pl.BlockSpec((tm, GLA_W), lambda i: (i, 0)),
            pl.BlockSpec((tm, SWA_QW), lambda i: (i, 0)),
            zblk(CONV_CH, Z_CX), zblk(CONV_CH, Z_CB), zblk(CONV_CH, Z_CC),
            halo(Z_CX), halo(Z_CC),
            zblk(D_MODEL, Z_GATES), zblk(D_MODEL, Z_GATES + D_MODEL), zblk(D_MODEL, Z_GATES + 2 * D_MODEL),
            _resident((3, CONV_CH)),
            _resident((GLA_W, D_MODEL)),
            _resident((CONV_CH, D_MODEL)),
            _resident((SWA_QW, D_MODEL)),
            _resident((D_MODEL, D_MODEL)),
        ],
        out_specs=pl.BlockSpec((tm, D_MODEL), lambda i: (i, 0)),
        out_shape=jax.ShapeDtypeStruct((t, D_MODEL), F32),
        compiler_params=_params(1),
        name="merge",
    )(h, o_gla, o_swa, z, z, z, z, z, z, z, z, conv_w, w_gla_o, w_conv_o, w_swa_o, w_o)


def _ffn_kernel(tiles_per_seq, final_norm, h_ref, g_ref, wg_ref, wv_ref, cwg_ref, cwv_ref, wd_ref,
                gf_ref, o_ref, tail_g_ref, tail_v_ref):
    @pl.when(pl.program_id(0) % tiles_per_seq == 0)
    def _():
        tail_g_ref[...] = jnp.zeros_like(tail_g_ref)
        tail_v_ref[...] = jnp.zeros_like(tail_v_ref)

    x = h_ref[...]
    u = _rms(x, g_ref[...]).astype(BF16)
    n = x.shape[0]
    o_ref[...] = x
    for c in range(wg_ref.shape[0]):
        hg = _dot(u, wg_ref[c])
        hv = _dot(u, wv_ref[c])
        prev_g = tail_g_ref[c]
        prev_v = tail_v_ref[c]
        tail_g_ref[c] = hg[n - SUBLANE:, :]
        tail_v_ref[c] = hv[n - SUBLANE:, :]
        cg = _causal_conv3(hg, prev_g, cwg_ref[c])
        cv = _causal_conv3(hv, prev_v, cwv_ref[c])
        act = (cg * _sigmoid(cg) * cv).astype(BF16)
        o_ref[...] += _dot(act, wd_ref[c])
    if final_norm:
        o_ref[...] = _rms(o_ref[...], gf_ref[...])


def _ffn(h, g, wg, wv, cwg, cwv, wd, g_final, seq_len, final_norm):
    t = h.shape[0]
    tm = TOKEN_TILE
    nf = wg.shape[0]
    return pl.pallas_call(
        functools.partial(_ffn_kernel, seq_len // tm, final_norm),
        grid=(t // tm,),
        in_specs=[
            pl.BlockSpec((tm, D_MODEL), lambda i: (i, 0)),
            _resident((1, D_MODEL)),
            _resident((nf, D_MODEL, FF_TILE)),
            _resident((nf, D_MODEL, FF_TILE)),
            _resident((nf, 3, FF_TILE)),
            _resident((nf, 3, FF_TILE)),
            _resident((nf, FF_TILE, D_MODEL)),
            _resident((1, D_MODEL)),
        ],
        out_specs=pl.BlockSpec((tm, D_MODEL), lambda i: (i, 0)),
        out_shape=jax.ShapeDtypeStruct((t, D_MODEL), F32),
        scratch_shapes=[pltpu.VMEM((nf, SUBLANE, FF_TILE), F32),
                        pltpu.VMEM((nf, SUBLANE, FF_TILE), F32)],
        compiler_params=_params(1),
        name="ffn",
    )(h, g, wg, wv, cwg, cwv, wd, g_final)


def _col_tiles(w, width):
    k, n = w.shape
    return jnp.transpose(w.reshape(k, n // width, width), (1, 0, 2))


def _prep_layer(w_in, w_alpha, w_up, ffn_conv_w, w_down):
    w_main = jnp.concatenate(
        [w_in[:, _SRC_GLA:_SRC_GA], w_in[:, _SRC_GATES:_SRC_END],
         w_in[:, _SRC_CONV:_SRC_SWA], w_in[:, _SRC_SWA:_SRC_GATES]], axis=1).astype(BF16)
    w_ga = jnp.pad(w_in[:, _SRC_GA:_SRC_CONV], ((0, 0), (0, LANE - GLA_RANK))).astype(BF16)
    wa = jnp.pad(w_alpha, ((0, LANE - GLA_RANK), (0, 0)))
    wa_hi = wa.astype(BF16)
    wa_lo = (wa - wa_hi.astype(F32)).astype(BF16)
    wg = _col_tiles(w_up[:, :D_FF], FF_TILE).astype(BF16)
    wv = _col_tiles(w_up[:, D_FF:], FF_TILE).astype(BF16)
    cwg = _col_tiles(ffn_conv_w[:, :D_FF], FF_TILE)
    cwv = _col_tiles(ffn_conv_w[:, D_FF:], FF_TILE)
    wd = w_down.reshape(D_FF // FF_TILE, FF_TILE, D_MODEL).astype(BF16)
    return w_main, w_ga, wa_hi, wa_lo, wg, wv, cwg, cwv, wd


def kernel(x, g_mix, w_in, gla_w_alpha, gla_b_alpha, gla_norm_g, conv_w, swa_sinks, w_gla_o,
           w_conv_o, w_swa_o, w_o, g_ffn, w_up, ffn_conv_w, w_down, g_final):
    bsz, seq_len, d = x.shape
    assert d == D_MODEL and seq_len % max(TOKEN_TILE, MIX_TILE) == 0
    depth = w_in.shape[0]
    h = x.reshape(bsz * seq_len, d)
    for l in range(depth):
        w_main, w_ga, wa_hi, wa_lo, wg, wv, cwg, cwv, wd = _prep_layer(
            w_in[l], gla_w_alpha[l], w_up[l], ffn_conv_w[l], w_down[l])
        z, la = _inproj(h, g_mix[l][None, :], w_main, w_ga, wa_hi, wa_lo, gla_b_alpha[l][None, :])
        o_gla = _gla(z, la, gla_norm_g[l][None, :], seq_len)
        o_swa = _swa(z, swa_sinks[l], seq_len)
        h = _merge(h, o_gla, o_swa, z, conv_w[l], w_gla_o[l].astype(BF16), w_conv_o[l].astype(BF16),
                   w_swa_o[l].astype(BF16), w_o[l].astype(BF16), seq_len)
        h = _ffn(h, g_ffn[l][None, :], wg, wv, cwg, cwv, wd, g_final[None, :], seq_len,
                 final_norm=(l == depth - 1))
    return h.reshape(bsz, seq_len, d)
```

```python
import functools

import jax
import jax.numpy as jnp
from jax import lax
from jax.experimental import pallas as pl
from jax.experimental.pallas import tpu as pltpu

F32 = jnp.float32
BF16 = jnp.bfloat16

D_MODEL = 1024
GLA_HEADS = 4
GLA_DK = 128
GLA_DV = 128
GLA_RANK = 16
GLA_TAU = 16.0
GLA_CHUNK = 64
CONV_CH = D_MODEL // 2
SWA_Q_HEADS = 8
SWA_KV_HEADS = 2
SWA_GROUP = SWA_Q_HEADS // SWA_KV_HEADS
SWA_HEAD_DIM = 64
SWA_WINDOW = 128
D_FF = 2816
EPS = 1e-6

GLA_W = GLA_HEADS * GLA_DK
SWA_QW = SWA_Q_HEADS * SWA_HEAD_DIM
SWA_KW = SWA_KV_HEADS * SWA_HEAD_DIM

_SRC_GLA = 0
_SRC_GA = 4 * GLA_W
_SRC_CONV = _SRC_GA + GLA_RANK
_SRC_SWA = _SRC_CONV + 3 * CONV_CH
_SRC_GATES = _SRC_SWA + SWA_QW + 2 * SWA_KW
_SRC_END = _SRC_GATES + 3 * D_MODEL

Z_Q, Z_K, Z_V, Z_R = 0, GLA_W, 2 * GLA_W, 3 * GLA_W
Z_GATES = 4 * GLA_W
Z_CX = Z_GATES + 3 * D_MODEL
Z_CB = Z_CX + CONV_CH
Z_CC = Z_CB + CONV_CH
Z_SQ = Z_CC + CONV_CH
Z_SK = Z_SQ + SWA_QW
Z_SV = Z_SK + SWA_KW
NZ = Z_SV + SWA_KW

LANE = 128
SUBLANE = 8
TOKEN_TILE = 512
MIX_TILE = 512
FF_TILE = 256
PROJ_CHUNK = 512
VMEM_LIMIT = 56 * 1024 * 1024


def _rms(x, g):
    return x * lax.rsqrt(jnp.mean(x * x, axis=-1, keepdims=True) + EPS) * g


def _dot(a, b):
    return jnp.dot(a, b, preferred_element_type=F32)


def _dot_nt(a, b):
    return lax.dot_general(a, b, (((1,), (1,)), ((), ())), preferred_element_type=F32)


def _dot_tn(a, b):
    return lax.dot_general(a, b, (((0,), (0,)), ((), ())), preferred_element_type=F32)


def _sigmoid(x):
    return 1.0 / (1.0 + jnp.exp(-x))


def _log_sigmoid(x):
    return jnp.minimum(x, 0.0) - jnp.log(1.0 + jnp.exp(-jnp.abs(x)))


def _split3(x):
    hi = x.astype(BF16)
    r1 = x - hi.astype(F32)
    mid = r1.astype(BF16)
    lo = (r1 - mid.astype(F32)).astype(BF16)
    return hi, mid, lo


def _resident(shape):
    nd = len(shape)
    return pl.BlockSpec(shape, lambda *_: (0,) * nd, pipeline_mode=pl.Buffered(1))


def _params(n_axes):
    return pltpu.CompilerParams(
        dimension_semantics=("arbitrary",) * n_axes, vmem_limit_bytes=VMEM_LIMIT)


def _inproj_kernel(h_ref, g_ref, w_ref, wga_ref, z_ref, ga_ref):
    u = _rms(h_ref[...], g_ref[...]).astype(BF16)
    for off in range(0, NZ, PROJ_CHUNK):
        wd = min(PROJ_CHUNK, NZ - off)
        z_ref[:, off:off + wd] = _dot(u, w_ref[:, off:off + wd]).astype(BF16)
    ga_ref[...] = _dot(u, wga_ref[...])


def _inproj(h, g, w_main, w_ga):
    t = h.shape[0]
    tm = TOKEN_TILE
    return pl.pallas_call(
        _inproj_kernel,
        grid=(t // tm,),
        in_specs=[
            pl.BlockSpec((tm, D_MODEL), lambda i: (i, 0)),
            _resident((1, D_MODEL)),
            _resident((D_MODEL, NZ)),
            _resident((D_MODEL, LANE)),
        ],
        out_specs=[
            pl.BlockSpec((tm, NZ), lambda i: (i, 0)),
            pl.BlockSpec((tm, LANE), lambda i: (i, 0)),
        ],
        out_shape=[
            jax.ShapeDtypeStruct((t, NZ), BF16),
            jax.ShapeDtypeStruct((t, LANE), F32),
        ],
        compiler_params=_params(1),
        name="inproj",
    )(h, g, w_main, w_ga)


def _gla_kernel(tiles_per_seq, q_ref, k_ref, v_ref, r_ref, ga_ref, wa_hi_ref, wa_lo_ref, ba_ref,
                g_ref, o_ref, st_ref):
    c_len = GLA_CHUNK
    n_chunks = o_ref.shape[0] // c_len

    @pl.when(pl.program_id(0) % tiles_per_seq == 0)
    def _():
        st_ref[...] = jnp.zeros_like(st_ref)

    row = lax.broadcasted_iota(jnp.int32, (c_len, c_len), 0)
    col = lax.broadcasted_iota(jnp.int32, (c_len, c_len), 1)
    causal = row >= col
    tril = jnp.where(causal, 1.0, 0.0).astype(BF16)
    scale = GLA_DK ** -0.5
    gain = g_ref[...]
    heads = range(GLA_HEADS)
    hsl = [slice(h * GLA_DK, (h + 1) * GLA_DK) for h in heads]
    rsl = [slice(c * c_len, (c + 1) * c_len) for c in range(n_chunks)]

    ga = ga_ref[...]
    ga_hi = ga.astype(BF16)
    ga_lo = (ga - ga_hi.astype(F32)).astype(BF16)
    pre = (_dot(ga_hi, wa_hi_ref[...]) + _dot(ga_lo, wa_hi_ref[...])
           + _dot(ga_hi, wa_lo_ref[...]) + ba_ref[...])
    la = _log_sigmoid(pre) * (1.0 / GLA_TAU)
    la_c = jnp.concatenate([la[rs, :] for rs in rsl], axis=1)
    la_hi, la_mid, la_lo = _split3(la_c)
    b_c = _dot(tril, la_hi) + _dot(tril, la_mid) + _dot(tril, la_lo)

    def scaled(c):
        b = b_c[:, c * GLA_W:(c + 1) * GLA_W]
        b_mid = b[c_len // 2 - 1:c_len // 2, :]
        b_last = b[c_len - 1:c_len, :]
        e_q = jnp.exp(b - b_mid)
        q_in = q_ref[rsl[c], :].astype(F32) * scale * e_q
        k_in = k_ref[rsl[c], :].astype(F32) / e_q
        return dict(
            q_in=q_in.astype(BF16),
            k_in=k_in.astype(BF16),
            q_ex=(q_in * jnp.exp(b_mid)).astype(BF16),
            k_ex=(k_in * jnp.exp(b_last - b_mid)).astype(BF16),
            decay=jnp.exp(b_last),
            v=v_ref[rsl[c], :])

    def scores(s):
        s["attn"] = [_dot_nt(s["q_in"][:, hs], s["k_in"][:, hs]) for hs in hsl]
        s["kv"] = [_dot_tn(s["v"][:, hs], s["k_ex"][:, hs]) for hs in hsl]

    def masked(s):
        s["attn"] = [jnp.where(causal, a, 0.0).astype(BF16) for a in s["attn"]]

    def intra(s):
        s["o"] = [_dot(a, s["v"][:, hs]) for a, hs in zip(s["attn"], hsl)]

    halves = [list(range(0, n_chunks // 2)), list(range(n_chunks // 2, n_chunks))]
    parts = {}
    for half in halves:
        for c in half:
            parts[c] = scaled(c)
        for c in half:
            scores(parts[c])
    for half in halves:
        for c in half:
            masked(parts[c])
        for c in half:
            intra(parts[c])

    states = [st_ref[h] for h in heads]
    for c in range(n_chunks):
        s = parts[c]
        inter = [_dot_nt(s["q_ex"][:, hs], states[h].astype(BF16)) for h, hs in enumerate(hsl)]
        states = [states[h] * s["decay"][:, hs] + s["kv"][h] for h, hs in enumerate(hsl)]
        for h, hs in enumerate(hsl):
            o = s["o"][h] + inter[h]
            o = o * lax.rsqrt(jnp.mean(o * o, axis=-1, keepdims=True) + EPS) * gain[:, hs]
            r = r_ref[rsl[c], hs].astype(F32)
            o_ref[rsl[c], hs] = (o * (r * _sigmoid(r))).astype(BF16)
    for h in heads:
        st_ref[h] = states[h]


def _gla(z, ga, wa_hi, wa_lo, b_alpha, gain, seq_len):
    t = z.shape[0]
    tl = MIX_TILE
    blk = lambda j: pl.BlockSpec((tl, GLA_W), lambda i: (i, j))
    return pl.pallas_call(
        functools.partial(_gla_kernel, seq_len // tl),
        grid=(t // tl,),
        in_specs=[blk(Z_Q // GLA_W), blk(Z_K // GLA_W), blk(Z_V // GLA_W), blk(Z_R // GLA_W),
                  pl.BlockSpec((tl, LANE), lambda i: (i, 0)),
                  _resident((LANE, GLA_W)), _resident((LANE, GLA_W)), _resident((1, GLA_W)),
                  _resident((1, GLA_W))],
        out_specs=pl.BlockSpec((tl, GLA_W), lambda i: (i, 0)),
        out_shape=jax.ShapeDtypeStruct((t, GLA_W), BF16),
        scratch_shapes=[pltpu.VMEM((GLA_HEADS, GLA_DV, GLA_DK), F32)],
        compiler_params=_params(1),
        name="gla",
    )(z, z, z, z, ga, wa_hi, wa_lo, b_alpha, gain)


def _swa_kernel(tiles_per_seq, sink_ref, q_ref, k_ref, v_ref, kp_ref, vp_ref, o_ref):
    w = SWA_WINDOW
    hd = SWA_HEAD_DIM
    grp = SWA_GROUP
    n_blk = q_ref.shape[0] // w
    seq_start = pl.program_id(0) % tiles_per_seq == 0
    scale = hd ** -0.5

    row = lax.broadcasted_iota(jnp.int32, (grp * w, 2 * w), 0)
    col = lax.broadcasted_iota(jnp.int32, (grp * w, 2 * w), 1)
    head_in_group = row // w
    dist = w + (row % w) - col
    valid = jnp.logical_and(dist >= 0, dist < w)
    no_prev = jnp.logical_and(seq_start, col < w)
    grp_col = lax.broadcasted_iota(jnp.int32, (grp * w, 1), 0) // w
    ones = jnp.ones((2 * w, hd), BF16)

    bias, bias_first, sinks, keys, vals = [], [], [], [], []
    for h in range(SWA_KV_HEADS):
        slope = jnp.zeros((grp * w, 2 * w), F32)
        sink = jnp.zeros((grp * w, 1), F32)
        for g in range(grp):
            hq = h * grp + g
            slope = jnp.where(head_in_group == g, 2.0 ** (-(8.0 / SWA_Q_HEADS) * (hq + 1)), slope)
            sink = jnp.where(grp_col == g, sink_ref[hq], sink)
        b = jnp.where(valid, -slope * dist.astype(F32), -jnp.inf)
        bias.append(b)
        bias_first.append(jnp.where(no_prev, -jnp.inf, b))
        sinks.append(sink)
        ks = slice(h * hd, (h + 1) * hd)
        keys.append(jnp.concatenate([kp_ref[:, ks], k_ref[:, ks]], axis=0))
        vals.append(jnp.concatenate([vp_ref[:, ks], v_ref[:, ks]], axis=0))

    def scores(h, j):
        qs = [q_ref[j * w:(j + 1) * w, (h * grp + g) * hd:(h * grp + g + 1) * hd] for g in range(grp)]
        q = (jnp.concatenate(qs, axis=0).astype(F32) * scale).astype(BF16)
        return _dot_nt(q, keys[h][j * w:(j + 2) * w, :])

    def finish(h, j, s):
        logits = s + (bias_first[h] if j == 0 else bias[h])
        m = jnp.maximum(jnp.max(logits, axis=-1, keepdims=True), sinks[h])
        p = jnp.exp(logits - m).astype(BF16)
        denom = _dot(p, ones) + jnp.exp(sinks[h] - m)
        o = _dot(p, vals[h][j * w:(j + 2) * w, :]) / denom
        for g in range(grp):
            hq = h * grp + g
            o_ref[j * w:(j + 1) * w, hq * hd:(hq + 1) * hd] = o[g * w:(g + 1) * w, :].astype(BF16)

    jobs = [(h, j) for h in range(SWA_KV_HEADS) for j in range(n_blk)]
    s_next = scores(*jobs[0])
    for idx, (h, j) in enumerate(jobs):
        s = s_next
        if idx + 1 < len(jobs):
            s_next = scores(*jobs[idx + 1])
        finish(h, j, s)


def _swa(z, sinks, seq_len):
    t = z.shape[0]
    tl = MIX_TILE
    w = SWA_WINDOW
    prev = lambda j: pl.BlockSpec((w, SWA_KW), lambda i: (jnp.maximum(i * (tl // w) - 1, 0), j))
    return pl.pallas_call(
        functools.partial(_swa_kernel, seq_len // tl),
        grid=(t // tl,),
        in_specs=[
            pl.BlockSpec(memory_space=pltpu.SMEM),
            pl.BlockSpec((tl, SWA_QW), lambda i: (i, Z_SQ // SWA_QW)),
            pl.BlockSpec((tl, SWA_KW), lambda i: (i, Z_SK // SWA_KW)),
            pl.BlockSpec((tl, SWA_KW), lambda i: (i, Z_SV // SWA_KW)),
            prev(Z_SK // SWA_KW),
            prev(Z_SV // SWA_KW),
        ],
        out_specs=pl.BlockSpec((tl, SWA_QW), lambda i: (i, 0)),
        out_shape=jax.ShapeDtypeStruct((t, SWA_QW), BF16),
        compiler_params=_params(1),
        name="swa",
    )(sinks, z, z, z, z, z)


def _causal_conv3(y, prev, w):
    row = lax.broadcasted_iota(jnp.int32, y.shape, 0)
    p1 = prev[SUBLANE - 1:SUBLANE, :]
    p2 = prev[SUBLANE - 2:SUBLANE - 1, :]
    y1 = jnp.where(row == 0, p1, pltpu.roll(y, 1, 0))
    y2 = jnp.where(row == 0, p2, jnp.where(row == 1, p1, pltpu.roll(y, 2, 0)))
    return w[0:1, :] * y2 + w[1:2, :] * y1 + w[2:3, :] * y


def _merge_kernel(tiles_per_seq, h_ref, og_ref, os_ref, cx_ref, cb_ref, cc_ref, cxp_ref, ccp_ref,
                  g1_ref, g2_ref, g3_ref, cw_ref, wg_ref, wc_ref, ws_ref, wo_ref, o_ref):
    seq_start = pl.program_id(0) % tiles_per_seq == 0
    y_gla = _dot(og_ref[...], wg_ref[...])
    y_swa = _dot(os_ref[...], ws_ref[...])
    prod = cc_ref[...].astype(F32) * cx_ref[...].astype(F32)
    prev = cxp_ref[...].astype(F32) * ccp_ref[...].astype(F32)
    prev = jnp.where(seq_start, 0.0, prev)
    conv = cb_ref[...].astype(F32) * _causal_conv3(prod, prev, cw_ref[...])
    y_conv = _dot(conv.astype(BF16), wc_ref[...])
    merged = (_sigmoid(g1_ref[...].astype(F32)) * y_gla
              + _sigmoid(g2_ref[...].astype(F32)) * y_conv
              + _sigmoid(g3_ref[...].astype(F32)) * y_swa)
    o_ref[...] = h_ref[...] + _dot(merged.astype(BF16), wo_ref[...])


def _merge(h, o_gla, o_swa, z, conv_w, w_gla_o, w_conv_o, w_swa_o, w_o, seq_len):
    t = h.shape[0]
    tm = TOKEN_TILE
    zblk = lambda width, off: pl.BlockSpec((tm, width), lambda i: (i, off // width))
    halo = lambda off: pl.BlockSpec(
        (SUBLANE, CONV_CH), lambda i: (jnp.maximum(i * (tm // SUBLANE) - 1, 0), off // CONV_CH))
    return pl.pallas_call(
        functools.partial(_merge_kernel, seq_len // tm),
        grid=(t // tm,),
        in_specs=[
            pl.BlockSpec((tm, D_MODEL), lambda i: (i, 0)),
            pl.BlockSpec((tm, GLA_W), lambda i: (i, 0)),
            pl.BlockSpec((tm, SWA_QW), lambda i: (i, 0)),
            zblk(CONV_CH, Z_CX), zblk(CONV_CH, Z_CB), zblk(CONV_CH, Z_CC),
            halo(Z_CX), halo(Z_CC),
            zblk(D_MODEL, Z_GATES), zblk(D_MODEL, Z_GATES + D_MODEL), zblk(D_MODEL, Z_GATES + 2 * D_MODEL),
            _resident((3, CONV_CH)),
            _resident((GLA_W, D_MODEL)),
            _resident((CONV_CH, D_MODEL)),
            _resident((SWA_QW, D_MODEL)),
            _resident((D_MODEL, D_MODEL)),
        ],
        out_specs=pl.BlockSpec((tm, D_MODEL), lambda i: (i, 0)),
        out_shape=jax.ShapeDtypeStruct((t, D_MODEL), F32),
        compiler_params=_params(1),
        name="merge",
    )(h, o_gla, o_swa, z, z, z, z, z, z, z, z, conv_w, w_gla_o, w_conv_o, w_swa_o, w_o)


def _ffn_kernel(tiles_per_seq, final_norm, h_ref, g_ref, wup_ref, cw_ref, wd_ref, gf_ref, o_ref,
                act_ref, tail_ref, hs_ref):
    @pl.when(pl.program_id(0) % tiles_per_seq == 0)
    def _():
        tail_ref[...] = jnp.zeros_like(tail_ref)

    x = h_ref[...]
    u = _rms(x, g_ref[...]).astype(BF16)
    n = x.shape[0]
    nf = wup_ref.shape[0]

    def gated(c, hid):
        slot = c % 2
        hs_ref[slot, 0:SUBLANE, :] = tail_ref[c]
        hs_ref[slot, SUBLANE:SUBLANE + n, :] = hid
        tail_ref[c] = hid[n - SUBLANE:, :]
        w = cw_ref[c]
        y1 = hs_ref[slot, SUBLANE - 1:SUBLANE - 1 + n, :]
        y2 = hs_ref[slot, SUBLANE - 2:SUBLANE - 2 + n, :]
        cv = w[0:1, :] * y2 + w[1:2, :] * y1 + w[2:3, :] * hid
        g = cv[:, :FF_TILE]
        v = cv[:, FF_TILE:]
        act_ref[:, c * FF_TILE:(c + 1) * FF_TILE] = (g * _sigmoid(g) * v).astype(BF16)

    hid = _dot(u, wup_ref[0])
    for c in range(nf):
        nxt = _dot(u, wup_ref[c + 1]) if c + 1 < nf else None
        gated(c, hid)
        hid = nxt
    out = x + _dot(act_ref[...], wd_ref[...])
    if final_norm:
        out = _rms(out, gf_ref[...])
    o_ref[...] = out


def _ffn(h, g, wup, cw, wd, g_final, seq_len, final_norm):
    t = h.shape[0]
    tm = TOKEN_TILE
    nf = wup.shape[0]
    return pl.pallas_call(
        functools.partial(_ffn_kernel, seq_len // tm, final_norm),
        grid=(t // tm,),
        in_specs=[
            pl.BlockSpec((tm, D_MODEL), lambda i: (i, 0)),
            _resident((1, D_MODEL)),
            _resident((nf, D_MODEL, 2 * FF_TILE)),
            _resident((nf, 3, 2 * FF_TILE)),
            _resident((D_FF, D_MODEL)),
            _resident((1, D_MODEL)),
        ],
        out_specs=pl.BlockSpec((tm, D_MODEL), lambda i: (i, 0)),
        out_shape=jax.ShapeDtypeStruct((t, D_MODEL), F32),
        scratch_shapes=[pltpu.VMEM((tm, D_FF), BF16),
                        pltpu.VMEM((nf, SUBLANE, 2 * FF_TILE), F32),
                        pltpu.VMEM((2, SUBLANE + tm, 2 * FF_TILE), F32)],
        compiler_params=_params(1),
        name="ffn",
    )(h, g, wup, cw, wd, g_final)


def _col_tiles(w, width):
    k, n = w.shape
    return jnp.transpose(w.reshape(k, n // width, width), (1, 0, 2))


def _prep_layer(w_in, w_alpha, w_up, ffn_conv_w, w_down):
    w_main = jnp.concatenate(
        [w_in[:, _SRC_GLA:_SRC_GA], w_in[:, _SRC_GATES:_SRC_END],
         w_in[:, _SRC_CONV:_SRC_SWA], w_in[:, _SRC_SWA:_SRC_GATES]], axis=1).astype(BF16)
    w_ga = jnp.pad(w_in[:, _SRC_GA:_SRC_CONV], ((0, 0), (0, LANE - GLA_RANK))).astype(BF16)
    wa = jnp.pad(w_alpha, ((0, LANE - GLA_RANK), (0, 0)))
    wa_hi = wa.astype(BF16)
    wa_lo = (wa - wa_hi.astype(F32)).astype(BF16)
    wup = jnp.concatenate([_col_tiles(w_up[:, :D_FF], FF_TILE), _col_tiles(w_up[:, D_FF:], FF_TILE)],
                          axis=2).astype(BF16)
    cw = jnp.concatenate([_col_tiles(ffn_conv_w[:, :D_FF], FF_TILE),
                          _col_tiles(ffn_conv_w[:, D_FF:], FF_TILE)], axis=2)
    return w_main, w_ga, wa_hi, wa_lo, wup, cw, w_down.astype(BF16)


def kernel(x, g_mix, w_in, gla_w_alpha, gla_b_alpha, gla_norm_g, conv_w, swa_sinks, w_gla_o,
           w_conv_o, w_swa_o, w_o, g_ffn, w_up, ffn_conv_w, w_down, g_final):
    bsz, seq_len, d = x.shape
    assert d == D_MODEL and seq_len % max(TOKEN_TILE, MIX_TILE) == 0
    depth = w_in.shape[0]
    h = x.reshape(bsz * seq_len, d)
    for l in range(depth):
        w_main, w_ga, wa_hi, wa_lo, wup, cw, wd = _prep_layer(
            w_in[l], gla_w_alpha[l], w_up[l], ffn_conv_w[l], w_down[l])
        z, ga = _inproj(h, g_mix[l][None, :], w_main, w_ga)
        o_gla = _gla(z, ga, wa_hi, wa_lo, gla_b_alpha[l][None, :], gla_norm_g[l][None, :], seq_len)
        o_swa = _swa(z, swa_sinks[l], seq_len)
        h = _merge(h, o_gla, o_swa, z, conv_w[l], w_gla_o[l].astype(BF16), w_conv_o[l].astype(BF16),
                   w_swa_o[l].astype(BF16), w_o[l].astype(BF16), seq_len)
        h = _ffn(h, g_ffn[l][None, :], wup, cw, wd, g_final[None, :], seq_len,
                 final_norm=(l == depth - 1))
    return h.reshape(bsz, seq_len, d)
```

```python
import functools

import jax
import jax.numpy as jnp
from jax import lax
from jax.experimental import pallas as pl
from jax.experimental.pallas import tpu as pltpu

F32 = jnp.float32
BF16 = jnp.bfloat16

D_MODEL = 1024
GLA_HEADS = 4
GLA_DK = 128
GLA_DV = 128
GLA_RANK = 16
GLA_TAU = 16.0
GLA_CHUNK = 64
CONV_CH = D_MODEL // 2
SWA_Q_HEADS = 8
SWA_KV_HEADS = 2
SWA_GROUP = SWA_Q_HEADS // SWA_KV_HEADS
SWA_HEAD_DIM = 64
SWA_WINDOW = 128
D_FF = 2816
EPS = 1e-6

GLA_W = GLA_HEADS * GLA_DK
SWA_QW = SWA_Q_HEADS * SWA_HEAD_DIM
SWA_KW = SWA_KV_HEADS * SWA_HEAD_DIM

_SRC_GLA = 0
_SRC_GA = 4 * GLA_W
_SRC_CONV = _SRC_GA + GLA_RANK
_SRC_SWA = _SRC_CONV + 3 * CONV_CH
_SRC_GATES = _SRC_SWA + SWA_QW + 2 * SWA_KW
_SRC_END = _SRC_GATES + 3 * D_MODEL

Z_Q, Z_K, Z_V, Z_R = 0, GLA_W, 2 * GLA_W, 3 * GLA_W
Z_GATES = 4 * GLA_W
Z_CX = Z_GATES + 3 * D_MODEL
Z_CB = Z_CX + CONV_CH
Z_CC = Z_CB + CONV_CH
Z_SQ = Z_CC + CONV_CH
Z_SK = Z_SQ + SWA_QW
Z_SV = Z_SK + SWA_KW
NZ = Z_SV + SWA_KW

LANE = 128
SUBLANE = 8
TOKEN_TILE = 512
MIX_TILE = 1024
FF_TILE = 256
PROJ_CHUNK = 512
VMEM_LIMIT = 56 * 1024 * 1024


def _rms(x, g):
    return x * lax.rsqrt(jnp.mean(x * x, axis=-1, keepdims=True) + EPS) * g


def _dot(a, b):
    return jnp.dot(a, b, preferred_element_type=F32)


def _dot_nt(a, b):
    return lax.dot_general(a, b, (((1,), (1,)), ((), ())), preferred_element_type=F32)


def _dot_tn(a, b):
    return lax.dot_general(a, b, (((0,), (0,)), ((), ())), preferred_element_type=F32)


def _log_sigmoid(x):
    return jnp.minimum(x, 0.0) - jnp.log(1.0 + jnp.exp(-jnp.abs(x)))


def _split3(x):
    hi = x.astype(BF16)
    r1 = x - hi.astype(F32)
    mid = r1.astype(BF16)
    lo = (r1 - mid.astype(F32)).astype(BF16)
    return hi, mid, lo


def _resident(shape):
    nd = len(shape)
    return pl.BlockSpec(shape, lambda *_: (0,) * nd, pipeline_mode=pl.Buffered(1))


def _params(n_axes):
    return pltpu.CompilerParams(
        dimension_semantics=("arbitrary",) * n_axes, vmem_limit_bytes=VMEM_LIMIT)


def _inproj_kernel(h_ref, g_ref, w_ref, wga_ref, z_ref, ga_ref):
    u = _rms(h_ref[...], g_ref[...]).astype(BF16)
    for off in range(0, NZ, PROJ_CHUNK):
        wd = min(PROJ_CHUNK, NZ - off)
        z_ref[:, off:off + wd] = _dot(u, w_ref[:, off:off + wd]).astype(BF16)
    ga_ref[...] = _dot(u, wga_ref[...])


def _inproj(h, g, w_main, w_ga):
    t = h.shape[0]
    tm = TOKEN_TILE
    return pl.pallas_call(
        _inproj_kernel,
        grid=(t // tm,),
        in_specs=[
            pl.BlockSpec((tm, D_MODEL), lambda i: (i, 0)),
            _resident((1, D_MODEL)),
            _resident((D_MODEL, NZ)),
            _resident((D_MODEL, LANE)),
        ],
        out_specs=[
            pl.BlockSpec((tm, NZ), lambda i: (i, 0)),
            pl.BlockSpec((tm, LANE), lambda i: (i, 0)),
        ],
        out_shape=[
            jax.ShapeDtypeStruct((t, NZ), BF16),
            jax.ShapeDtypeStruct((t, LANE), F32),
        ],
        compiler_params=_params(1),
        name="inproj",
    )(h, g, w_main, w_ga)


def _gla_kernel(tiles_per_seq, q_ref, k_ref, v_ref, r_ref, ga_ref, wa_hi_ref, wa_lo_ref, ba_ref,
                g_ref, o_ref, st_ref):
    c_len = GLA_CHUNK
    n_chunks = o_ref.shape[0] // c_len

    @pl.when(pl.program_id(0) % tiles_per_seq == 0)
    def _():
        st_ref[...] = jnp.zeros_like(st_ref)

    row = lax.broadcasted_iota(jnp.int32, (c_len, c_len), 0)
    col = lax.broadcasted_iota(jnp.int32, (c_len, c_len), 1)
    causal = row >= col
    tril = jnp.where(causal, 1.0, 0.0).astype(BF16)
    scale = GLA_DK ** -0.5
    gain = g_ref[...]
    heads = range(GLA_HEADS)
    hsl = [slice(h * GLA_DK, (h + 1) * GLA_DK) for h in heads]
    rsl = [slice(c * c_len, (c + 1) * c_len) for c in range(n_chunks)]

    ga = ga_ref[...]
    ga_hi = ga.astype(BF16)
    ga_lo = (ga - ga_hi.astype(F32)).astype(BF16)
    pre = (_dot(ga_hi, wa_hi_ref[...]) + _dot(ga_lo, wa_hi_ref[...])
           + _dot(ga_hi, wa_lo_ref[...]) + ba_ref[...])
    la = _log_sigmoid(pre) * (1.0 / GLA_TAU)
    la_c = jnp.concatenate([la[rs, :] for rs in rsl], axis=1)
    la_hi, la_mid, la_lo = _split3(la_c)
    b_c = _dot(tril, la_hi) + _dot(tril, la_mid) + _dot(tril, la_lo)

    def scaled(c):
        b = b_c[:, c * GLA_W:(c + 1) * GLA_W]
        b_mid = b[c_len // 2 - 1:c_len // 2, :]
        b_last = b[c_len - 1:c_len, :]
        e_q = jnp.exp(b - b_mid)
        q_in = q_ref[rsl[c], :].astype(F32) * scale * e_q
        k_in = k_ref[rsl[c], :].astype(F32) / e_q
        return dict(
            q_in=q_in.astype(BF16),
            k_in=k_in.astype(BF16),
            q_ex=(q_in * jnp.exp(b_mid)).astype(BF16),
            k_ex=(k_in * jnp.exp(b_last - b_mid)).astype(BF16),
            decay=jnp.exp(b_last),
            v=v_ref[rsl[c], :])

    def scores(s):
        s["attn"] = [_dot_nt(s["q_in"][:, hs], s["k_in"][:, hs]) for hs in hsl]
        s["kv"] = [_dot_tn(s["v"][:, hs], s["k_ex"][:, hs]) for hs in hsl]

    def masked(s):
        s["attn"] = [jnp.where(causal, a, 0.0).astype(BF16) for a in s["attn"]]

    def intra(s):
        s["o"] = [_dot(a, s["v"][:, hs]) for a, hs in zip(s["attn"], hsl)]

    halves = [list(range(0, n_chunks // 2)), list(range(n_chunks // 2, n_chunks))]
    parts = {}
    for half in halves:
        for c in half:
            parts[c] = scaled(c)
        for c in half:
            scores(parts[c])
    for half in halves:
        for c in half:
            masked(parts[c])
        for c in half:
            intra(parts[c])

    states = [st_ref[h] for h in heads]
    for c in range(n_chunks):
        s = parts[c]
        inter = [_dot_nt(s["q_ex"][:, hs], states[h].astype(BF16)) for h, hs in enumerate(hsl)]
        states = [states[h] * s["decay"][:, hs] + s["kv"][h] for h, hs in enumerate(hsl)]
        for h, hs in enumerate(hsl):
            o = s["o"][h] + inter[h]
            o = o * lax.rsqrt(jnp.mean(o * o, axis=-1, keepdims=True) + EPS) * gain[:, hs]
            hr = 0.5 * r_ref[rsl[c], hs].astype(F32)
            o_ref[rsl[c], hs] = (o * (hr + hr * jnp.tanh(hr))).astype(BF16)
    for h in heads:
        st_ref[h] = states[h]


def _gla(z, ga, wa_hi, wa_lo, b_alpha, gain, seq_len):
    t = z.shape[0]
    tl = MIX_TILE
    blk = lambda j: pl.BlockSpec((tl, GLA_W), lambda i: (i, j))
    return pl.pallas_call(
        functools.partial(_gla_kernel, seq_len // tl),
        grid=(t // tl,),
        in_specs=[blk(Z_Q // GLA_W), blk(Z_K // GLA_W), blk(Z_V // GLA_W), blk(Z_R // GLA_W),
                  pl.BlockSpec((tl, LANE), lambda i: (i, 0)),
                  _resident((LANE, GLA_W)), _resident((LANE, GLA_W)), _resident((1, GLA_W)),
                  _resident((1, GLA_W))],
        out_specs=pl.BlockSpec((tl, GLA_W), lambda i: (i, 0)),
        out_shape=jax.ShapeDtypeStruct((t, GLA_W), BF16),
        scratch_shapes=[pltpu.VMEM((GLA_HEADS, GLA_DV, GLA_DK), F32)],
        compiler_params=_params(1),
        name="gla",
    )(z, z, z, z, ga, wa_hi, wa_lo, b_alpha, gain)


def _swa_kernel(tiles_per_seq, sink_ref, q_ref, k_ref, v_ref, kp_ref, vp_ref, o_ref):
    w = SWA_WINDOW
    hd = SWA_HEAD_DIM
    grp = SWA_GROUP
    n_blk = q_ref.shape[0] // w
    seq_start = pl.program_id(0) % tiles_per_seq == 0
    scale = hd ** -0.5

    row = lax.broadcasted_iota(jnp.int32, (grp * w, 2 * w), 0)
    col = lax.broadcasted_iota(jnp.int32, (grp * w, 2 * w), 1)
    head_in_group = row // w
    dist = w + (row % w) - col
    valid = jnp.logical_and(dist >= 0, dist < w)
    no_prev = jnp.logical_and(seq_start, col < w)
    grp_col = lax.broadcasted_iota(jnp.int32, (grp * w, 1), 0) // w
    ones = jnp.ones((2 * w, hd), BF16)

    bias, bias_first, sinks, keys, vals = [], [], [], [], []
    for h in range(SWA_KV_HEADS):
        slope = jnp.zeros((grp * w, 2 * w), F32)
        sink = jnp.zeros((grp * w, 1), F32)
        for g in range(grp):
            hq = h * grp + g
            slope = jnp.where(head_in_group == g, 2.0 ** (-(8.0 / SWA_Q_HEADS) * (hq + 1)), slope)
            sink = jnp.where(grp_col == g, sink_ref[hq], sink)
        b = jnp.where(valid, -slope * dist.astype(F32), -jnp.inf)
        bias.append(b)
        bias_first.append(jnp.where(no_prev, -jnp.inf, b))
        sinks.append(sink)
        ks = slice(h * hd, (h + 1) * hd)
        keys.append(jnp.concatenate([kp_ref[:, ks], k_ref[:, ks]], axis=0))
        vals.append(jnp.concatenate([vp_ref[:, ks], v_ref[:, ks]], axis=0))

    def scores(h, j):
        qs = [q_ref[j * w:(j + 1) * w, (h * grp + g) * hd:(h * grp + g + 1) * hd] for g in range(grp)]
        q = (jnp.concatenate(qs, axis=0).astype(F32) * scale).astype(BF16)
        return _dot_nt(q, keys[h][j * w:(j + 2) * w, :])

    def finish(h, j, s):
        logits = s + (bias_first[h] if j == 0 else bias[h])
        m = jnp.maximum(jnp.max(logits, axis=-1, keepdims=True), sinks[h])
        p = jnp.exp(logits - m).astype(BF16)
        denom = _dot(p, ones) + jnp.exp(sinks[h] - m)
        o = _dot(p, vals[h][j * w:(j + 2) * w, :]) / denom
        for g in range(grp):
            hq = h * grp + g
            o_ref[j * w:(j + 1) * w, hq * hd:(hq + 1) * hd] = o[g * w:(g + 1) * w, :].astype(BF16)

    jobs = [(h, j) for h in range(SWA_KV_HEADS) for j in range(n_blk)]
    s_next = scores(*jobs[0])
    for idx, (h, j) in enumerate(jobs):
        s = s_next
        if idx + 1 < len(jobs):
            s_next = scores(*jobs[idx + 1])
        finish(h, j, s)


def _swa(z, sinks, seq_len):
    t = z.shape[0]
    tl = MIX_TILE
    w = SWA_WINDOW
    prev = lambda j: pl.BlockSpec((w, SWA_KW), lambda i: (jnp.maximum(i * (tl // w) - 1, 0), j))
    return pl.pallas_call(
        functools.partial(_swa_kernel, seq_len // tl),
        grid=(t // tl,),
        in_specs=[
            pl.BlockSpec(memory_space=pltpu.SMEM),
            pl.BlockSpec((tl, SWA_QW), lambda i: (i, Z_SQ // SWA_QW)),
            pl.BlockSpec((tl, SWA_KW), lambda i: (i, Z_SK // SWA_KW)),
            pl.BlockSpec((tl, SWA_KW), lambda i: (i, Z_SV // SWA_KW)),
            prev(Z_SK // SWA_KW),
            prev(Z_SV // SWA_KW),
        ],
        out_specs=pl.BlockSpec((tl, SWA_QW), lambda i: (i, 0)),
        out_shape=jax.ShapeDtypeStruct((t, SWA_QW), BF16),
        compiler_params=_params(1),
        name="swa",
    )(sinks, z, z, z, z, z)


def _causal_conv3(y, prev, w):
    n = y.shape[0]
    ext = jnp.concatenate([prev, y], axis=0)
    y1 = ext[SUBLANE - 1:SUBLANE - 1 + n, :]
    y2 = ext[SUBLANE - 2:SUBLANE - 2 + n, :]
    return w[0:1, :] * y2 + w[1:2, :] * y1 + w[2:3, :] * y


def _merge_kernel(tiles_per_seq, h_ref, og_ref, os_ref, cx_ref, cb_ref, cc_ref, cxp_ref, ccp_ref,
                  g1_ref, g2_ref, g3_ref, cw_ref, wg_ref, wc_ref, ws_ref, wo_ref, o_ref):
    seq_start = pl.program_id(0) % tiles_per_seq == 0
    y_gla = _dot(og_ref[...], wg_ref[...])
    y_swa = _dot(os_ref[...], ws_ref[...])
    prod = cc_ref[...].astype(F32) * cx_ref[...].astype(F32)
    prev = cxp_ref[...].astype(F32) * ccp_ref[...].astype(F32)
    prev = jnp.where(seq_start, 0.0, prev)
    conv = cb_ref[...].astype(F32) * _causal_conv3(prod, prev, cw_ref[...])
    y_conv = _dot(conv.astype(BF16), wc_ref[...])
    def gate(g_ref, y):
        return y + jnp.tanh((g_ref[...] * 0.5).astype(F32)) * y

    merged = 0.5 * (gate(g1_ref, y_gla) + gate(g2_ref, y_conv) + gate(g3_ref, y_swa))
    o_ref[...] = h_ref[...] + _dot(merged.astype(BF16), wo_ref[...])


def _merge(h, o_gla, o_swa, z, conv_w, w_gla_o, w_conv_o, w_swa_o, w_o, seq_len):
    t = h.shape[0]
    tm = TOKEN_TILE
    zblk = lambda width, off: pl.BlockSpec((tm, width), lambda i: (i, off // width))
    halo = lambda off: pl.BlockSpec(
        (SUBLANE, CONV_CH), lambda i: (jnp.maximum(i * (tm // SUBLANE) - 1, 0), off // CONV_CH))
    return pl.pallas_call(
        functools.partial(_merge_kernel, seq_len // tm),
        grid=(t // tm,),
        in_specs=[
            pl.BlockSpec((tm, D_MODEL), lambda i: (i, 0)),
            pl.BlockSpec((tm, GLA_W), lambda i: (i, 0)),
            pl.BlockSpec((tm, SWA_QW), lambda i: (i, 0)),
            zblk(CONV_CH, Z_CX), zblk(CONV_CH, Z_CB), zblk(CONV_CH, Z_CC),
            halo(Z_CX), halo(Z_CC),
            zblk(D_MODEL, Z_GATES), zblk(D_MODEL, Z_GATES + D_MODEL), zblk(D_MODEL, Z_GATES + 2 * D_MODEL),
            _resident((3, CONV_CH)),
            _resident((GLA_W, D_MODEL)),
            _resident((CONV_CH, D_MODEL)),
            _resident((SWA_QW, D_MODEL)),
            _resident((D_MODEL, D_MODEL)),
        ],
        out_specs=pl.BlockSpec((tm, D_MODEL), lambda i: (i, 0)),
        out_shape=jax.ShapeDtypeStruct((t, D_MODEL), F32),
        compiler_params=_params(1),
        name="merge",
    )(h, o_gla, o_swa, z, z, z, z, z, z, z, z, conv_w, w_gla_o, w_conv_o, w_swa_o, w_o)


def _ffn_kernel(tiles_per_seq, final_norm, h_ref, g_ref, wup_ref, cw_ref, wd_ref, gf_ref, o_ref,
                act_ref, tail_ref):
    @pl.when(pl.program_id(0) % tiles_per_seq == 0)
    def _():
        tail_ref[...] = jnp.zeros_like(tail_ref)

    x = h_ref[...]
    u = _rms(x, g_ref[...]).astype(BF16)
    n = x.shape[0]
    nf = D_FF // FF_TILE
    gcol = lambda c: slice(c * FF_TILE, (c + 1) * FF_TILE)
    vcol = lambda c: slice(D_FF + c * FF_TILE, D_FF + (c + 1) * FF_TILE)

    def up(c):
        return _dot(u, wup_ref[:, gcol(c)]), _dot(u, wup_ref[:, vcol(c)])

    def gated(c, hid):
        hg, hv = hid
        pg = tail_ref[0, :, gcol(c)]
        pv = tail_ref[1, :, gcol(c)]
        tail_ref[0, :, gcol(c)] = hg[n - SUBLANE:, :]
        tail_ref[1, :, gcol(c)] = hv[n - SUBLANE:, :]
        s = _causal_conv3(hg, pg, 0.5 * cw_ref[:, gcol(c)])
        v = _causal_conv3(hv, pv, cw_ref[:, vcol(c)])
        act_ref[:, gcol(c)] = ((s + s * jnp.tanh(s)) * v).astype(BF16)

    hid = up(0)
    for c in range(nf):
        nxt = up(c + 1) if c + 1 < nf else None
        gated(c, hid)
        hid = nxt
    out = x + _dot(act_ref[...], wd_ref[...])
    if final_norm:
        out = _rms(out, gf_ref[...])
    o_ref[...] = out


def _ffn(h, g, wup, cw, wd, g_final, seq_len, final_norm):
    t = h.shape[0]
    tm = TOKEN_TILE
    return pl.pallas_call(
        functools.partial(_ffn_kernel, seq_len // tm, final_norm),
        grid=(t // tm,),
        in_specs=[
            pl.BlockSpec((tm, D_MODEL), lambda i: (i, 0)),
            _resident((1, D_MODEL)),
            _resident((D_MODEL, 2 * D_FF)),
            _resident((3, 2 * D_FF)),
            _resident((D_FF, D_MODEL)),
            _resident((1, D_MODEL)),
        ],
        out_specs=pl.BlockSpec((tm, D_MODEL), lambda i: (i, 0)),
        out_shape=jax.ShapeDtypeStruct((t, D_MODEL), F32),
        scratch_shapes=[pltpu.VMEM((tm, D_FF), BF16),
                        pltpu.VMEM((2, SUBLANE, D_FF), F32)],
        compiler_params=_params(1),
        name="ffn",
    )(h, g, wup, cw, wd, g_final)


def _prep_layer(w_in, w_alpha):
    w_main = jnp.concatenate(
        [w_in[:, _SRC_GLA:_SRC_GA], w_in[:, _SRC_GATES:_SRC_END],
         w_in[:, _SRC_CONV:_SRC_SWA], w_in[:, _SRC_SWA:_SRC_GATES]], axis=1).astype(BF16)
    w_ga = jnp.pad(w_in[:, _SRC_GA:_SRC_CONV], ((0, 0), (0, LANE - GLA_RANK))).astype(BF16)
    wa = jnp.pad(w_alpha, ((0, LANE - GLA_RANK), (0, 0)))
    wa_hi = wa.astype(BF16)
    wa_lo = (wa - wa_hi.astype(F32)).astype(BF16)
    return w_main, w_ga, wa_hi, wa_lo


def kernel(x, g_mix, w_in, gla_w_alpha, gla_b_alpha, gla_norm_g, conv_w, swa_sinks, w_gla_o,
           w_conv_o, w_swa_o, w_o, g_ffn, w_up, ffn_conv_w, w_down, g_final):
    bsz, seq_len, d = x.shape
    assert d == D_MODEL and seq_len % max(TOKEN_TILE, MIX_TILE) == 0
    depth = w_in.shape[0]
    h = x.reshape(bsz * seq_len, d)
    for l in range(depth):
        w_main, w_ga, wa_hi, wa_lo = _prep_layer(w_in[l], gla_w_alpha[l])
        z, ga = _inproj(h, g_mix[l][None, :], w_main, w_ga)
        o_gla = _gla(z, ga, wa_hi, wa_lo, gla_b_alpha[l][None, :], gla_norm_g[l][None, :], seq_len)
        o_swa = _swa(z, swa_sinks[l], seq_len)
        h = _merge(h, o_gla, o_swa, z, conv_w[l], w_gla_o[l].astype(BF16), w_conv_o[l].astype(BF16),
                   w_swa_o[l].astype(BF16), w_o[l].astype(BF16), seq_len)
        h = _ffn(h, g_ffn[l][None, :], w_up[l].astype(BF16), ffn_conv_w[l], w_down[l].astype(BF16),
                 g_final[None, :], seq_len, final_norm=(l == depth - 1))
    return h.reshape(bsz, seq_len, d)
```

```python
import functools

import jax
import jax.numpy as jnp
from jax import lax
from jax.experimental import pallas as pl
from jax.experimental.pallas import tpu as pltpu

F32 = jnp.float32
BF16 = jnp.bfloat16

D_MODEL = 1024
GLA_HEADS = 4
GLA_DK = 128
GLA_DV = 128
GLA_RANK = 16
GLA_TAU = 16.0
GLA_CHUNK = 64
CONV_CH = D_MODEL // 2
SWA_Q_HEADS = 8
SWA_KV_HEADS = 2
SWA_GROUP = SWA_Q_HEADS // SWA_KV_HEADS
SWA_HEAD_DIM = 64
SWA_WINDOW = 128
D_FF = 2816
EPS = 1e-6

GLA_W = GLA_HEADS * GLA_DK
SWA_QW = SWA_Q_HEADS * SWA_HEAD_DIM
SWA_KW = SWA_KV_HEADS * SWA_HEAD_DIM

_SRC_GLA = 0
_SRC_GA = 4 * GLA_W
_SRC_CONV = _SRC_GA + GLA_RANK
_SRC_SWA = _SRC_CONV + 3 * CONV_CH
_SRC_GATES = _SRC_SWA + SWA_QW + 2 * SWA_KW
_SRC_END = _SRC_GATES + 3 * D_MODEL

Z_GATES = 0
Z_CX = 3 * D_MODEL
Z_CB = Z_CX + CONV_CH
Z_CC = Z_CB + CONV_CH
ZC = Z_CC + CONV_CH
M_Q, M_K, M_V, M_R = 0, GLA_W, 2 * GLA_W, 3 * GLA_W
M_SQ = 4 * GLA_W
M_SK = M_SQ + SWA_QW
M_SV = M_SK + SWA_KW
ZM = M_SV + SWA_KW
NZ = ZC + ZM

LANE = 128
SUBLANE = 8
TOKEN_TILE = 512
FF_TILE = 256
PROJ_CHUNK = 512
VMEM_LIMIT = 56 * 1024 * 1024


def _rms(x, g):
    return x * lax.rsqrt(jnp.mean(x * x, axis=-1, keepdims=True) + EPS) * g


def _dot(a, b):
    return jnp.dot(a, b, preferred_element_type=F32)


def _dot_nt(a, b):
    return lax.dot_general(a, b, (((1,), (1,)), ((), ())), preferred_element_type=F32)


def _dot_tn(a, b):
    return lax.dot_general(a, b, (((0,), (0,)), ((), ())), preferred_element_type=F32)


def _log_sigmoid(x):
    return jnp.minimum(x, 0.0) - jnp.log(1.0 + jnp.exp(-jnp.abs(x)))


def _split3(x):
    hi = x.astype(BF16)
    r1 = x - hi.astype(F32)
    mid = r1.astype(BF16)
    lo = (r1 - mid.astype(F32)).astype(BF16)
    return hi, mid, lo


def _resident(shape):
    nd = len(shape)
    return pl.BlockSpec(shape, lambda *_: (0,) * nd, pipeline_mode=pl.Buffered(1))


def _params(n_axes):
    return pltpu.CompilerParams(
        dimension_semantics=("arbitrary",) * n_axes, vmem_limit_bytes=VMEM_LIMIT)


def _gla_stages(seq_start, q_ref, k_ref, v_ref, r_ref, ga_ref, wa_hi_ref, wa_lo_ref, ba_ref,
                g_ref, o_ref, st_ref):
    c_len = GLA_CHUNK
    n_chunks = o_ref.shape[0] // c_len

    @pl.when(seq_start)
    def _():
        st_ref[...] = jnp.zeros_like(st_ref)

    row = lax.broadcasted_iota(jnp.int32, (c_len, c_len), 0)
    col = lax.broadcasted_iota(jnp.int32, (c_len, c_len), 1)
    causal = row >= col
    tril = jnp.where(causal, 1.0, 0.0).astype(BF16)
    scale = GLA_DK ** -0.5
    gain = g_ref[...]
    heads = range(GLA_HEADS)
    hsl = [slice(h * GLA_DK, (h + 1) * GLA_DK) for h in heads]
    rsl = [slice(c * c_len, (c + 1) * c_len) for c in range(n_chunks)]

    ga = ga_ref[...]
    ga_hi = ga.astype(BF16)
    ga_lo = (ga - ga_hi.astype(F32)).astype(BF16)
    pre = (_dot(ga_hi, wa_hi_ref[...]) + _dot(ga_lo, wa_hi_ref[...])
           + _dot(ga_hi, wa_lo_ref[...]) + ba_ref[...])
    yield
    la = _log_sigmoid(pre) * (1.0 / GLA_TAU)
    la_c = jnp.concatenate([la[rs, :] for rs in rsl], axis=1)
    la_hi, la_mid, la_lo = _split3(la_c)
    yield
    b_c = _dot(tril, la_hi) + _dot(tril, la_mid) + _dot(tril, la_lo)
    yield

    def scaled(c):
        b = b_c[:, c * GLA_W:(c + 1) * GLA_W]
        b_mid = b[c_len // 2 - 1:c_len // 2, :]
        b_last = b[c_len - 1:c_len, :]
        e_q = jnp.exp(b - b_mid)
        q_in = q_ref[rsl[c], :].astype(F32) * scale * e_q
        k_in = k_ref[rsl[c], :].astype(F32) / e_q
        return dict(
            q_in=q_in.astype(BF16),
            k_in=k_in.astype(BF16),
            q_ex=(q_in * jnp.exp(b_mid)).astype(BF16),
            k_ex=(k_in * jnp.exp(b_last - b_mid)).astype(BF16),
            decay=jnp.exp(b_last),
            v=v_ref[rsl[c], :])

    def scores(s):
        s["attn"] = [_dot_nt(s["q_in"][:, hs], s["k_in"][:, hs]) for hs in hsl]
        s["kv"] = [_dot_tn(s["v"][:, hs], s["k_ex"][:, hs]) for hs in hsl]

    def masked(s):
        s["attn"] = [jnp.where(causal, a, 0.0).astype(BF16) for a in s["attn"]]

    def intra(s):
        s["o"] = [_dot(a, s["v"][:, hs]) for a, hs in zip(s["attn"], hsl)]

    quarters = [list(range(q * n_chunks // 4, (q + 1) * n_chunks // 4)) for q in range(4)]
    parts = {}
    for qt in quarters:
        for c in qt:
            parts[c] = scaled(c)
        yield
        for c in qt:
            scores(parts[c])
    yield
    for qt in quarters:
        for c in qt:
            masked(parts[c])
        yield
        for c in qt:
            intra(parts[c])
    yield

    states = [st_ref[h] for h in heads]
    for c in range(n_chunks):
        s = parts[c]
        inter = [_dot_nt(s["q_ex"][:, hs], states[h].astype(BF16)) for h, hs in enumerate(hsl)]
        states = [states[h] * s["decay"][:, hs] + s["kv"][h] for h, hs in enumerate(hsl)]
        if c % 2 == 0:
            yield
        for h, hs in enumerate(hsl):
            o = s["o"][h] + inter[h]
            o = o * lax.rsqrt(jnp.mean(o * o, axis=-1, keepdims=True) + EPS) * gain[:, hs]
            hr = 0.5 * r_ref[rsl[c], hs].astype(F32)
            o_ref[rsl[c], hs] = (o * (hr + hr * jnp.tanh(hr))).astype(BF16)
    for h in heads:
        st_ref[h] = states[h]


def _swa_stages(seq_start, sink_ref, q_ref, k_ref, v_ref, kp, vp, o_ref):
    w = SWA_WINDOW
    hd = SWA_HEAD_DIM
    grp = SWA_GROUP
    n_blk = q_ref.shape[0] // w
    scale = hd ** -0.5

    row = lax.broadcasted_iota(jnp.int32, (grp * w, 2 * w), 0)
    col = lax.broadcasted_iota(jnp.int32, (grp * w, 2 * w), 1)
    head_in_group = row // w
    dist = w + (row % w) - col
    valid = jnp.logical_and(dist >= 0, dist < w)
    no_prev = jnp.logical_and(seq_start, col < w)
    grp_col = lax.broadcasted_iota(jnp.int32, (grp * w, 1), 0) // w

    bias, bias_first, sinks, keys, vals = [], [], [], [], []
    for h in range(SWA_KV_HEADS):
        slope = jnp.zeros((grp * w, 2 * w), F32)
        sink = jnp.zeros((grp * w, 1), F32)
        for g in range(grp):
            hq = h * grp + g
            slope = jnp.where(head_in_group == g, 2.0 ** (-(8.0 / SWA_Q_HEADS) * (hq + 1)), slope)
            sink = jnp.where(grp_col == g, sink_ref[hq], sink)
        b = jnp.where(valid, -slope * dist.astype(F32), -jnp.inf)
        bias.append(b)
        bias_first.append(jnp.where(no_prev, -jnp.inf, b))
        sinks.append(sink)
        ks = slice(h * hd, (h + 1) * hd)
        keys.append(jnp.concatenate([kp[:, ks], k_ref[:, ks]], axis=0))
        v_h = jnp.concatenate([vp[:, ks], v_ref[:, ks]], axis=0)
        vals.append(jnp.concatenate([v_h, jnp.ones_like(v_h)], axis=1))
    yield

    def scores(h, j):
        qs = [q_ref[j * w:(j + 1) * w, (h * grp + g) * hd:(h * grp + g + 1) * hd] for g in range(grp)]
        q = (jnp.concatenate(qs, axis=0).astype(F32) * scale).astype(BF16)
        return _dot_nt(q, keys[h][j * w:(j + 2) * w, :])

    def finish(h, j, s):
        logits = s + (bias_first[h] if j == 0 else bias[h])
        m = jnp.maximum(jnp.max(logits, axis=-1, keepdims=True), sinks[h])
        p = jnp.exp(logits - m).astype(BF16)
        ov = _dot(p, vals[h][j * w:(j + 2) * w, :])
        o = ov[:, :hd] / (ov[:, hd:] + jnp.exp(sinks[h] - m))
        for g in range(grp):
            hq = h * grp + g
            o_ref[j * w:(j + 1) * w, hq * hd:(hq + 1) * hd] = o[g * w:(g + 1) * w, :].astype(BF16)

    jobs = [(h, j) for h in range(SWA_KV_HEADS) for j in range(n_blk)]
    s_next = scores(*jobs[0])
    for idx, (h, j) in enumerate(jobs):
        s = s_next
        if idx + 1 < len(jobs):
            s_next = scores(*jobs[idx + 1])
        yield
        finish(h, j, s)


def _front_kernel(tiles_per_seq, h_ref, g_ref, w_ref, wga_ref, wa_hi_ref, wa_lo_ref, ba_ref, gn_ref, sink_ref,
                  zc_ref, og_ref, os_ref, zm_ref, ga_scr, kvt_ref, st_ref):
    i = pl.program_id(0)
    cur = i % 2
    prv = 1 - cur
    n = h_ref.shape[0]

    @pl.when(i == 0)
    def _():
        zm_ref[1] = jnp.zeros(zm_ref.shape[1:], zm_ref.dtype)
        ga_scr[1] = jnp.zeros(ga_scr.shape[1:], ga_scr.dtype)
        kvt_ref[...] = jnp.zeros_like(kvt_ref)

    u = _rms(h_ref[...], g_ref[...]).astype(BF16)
    seq_start = jnp.maximum(i - 1, 0) % tiles_per_seq == 0
    zprev = lambda off, width: zm_ref.at[prv, :, off:off + width]
    kv_tail = kvt_ref[...]
    mixers = [
        _gla_stages(seq_start, zprev(M_Q, GLA_W), zprev(M_K, GLA_W), zprev(M_V, GLA_W), zprev(M_R, GLA_W),
                    ga_scr.at[prv], wa_hi_ref, wa_lo_ref, ba_ref, gn_ref, og_ref, st_ref),
        _swa_stages(seq_start, sink_ref, zprev(M_SQ, SWA_QW), zprev(M_SK, SWA_KW), zprev(M_SV, SWA_KW),
                    kv_tail[:, :SWA_KW], kv_tail[:, SWA_KW:], os_ref),
    ]

    def advance():
        for gen in list(mixers):
            try:
                next(gen)
            except StopIteration:
                mixers.remove(gen)

    for off in range(0, NZ, PROJ_CHUNK):
        wd = min(PROJ_CHUNK, NZ - off)
        res = _dot(u, w_ref[:, off:off + wd]).astype(BF16)
        if off < ZC:
            zc_ref[:, off:off + wd] = res
        else:
            zm_ref[cur, :, off - ZC:off - ZC + wd] = res
        advance()
    ga_scr[cur] = _dot(u, wga_ref[...])
    while mixers:
        advance()
    kvt_ref[...] = zm_ref[prv, n - SWA_WINDOW:, M_SK:M_SK + 2 * SWA_KW]


def _front(h, g, w_main, w_ga, wa_hi, wa_lo, b_alpha, gain, sinks, seq_len):
    t = h.shape[0]
    tm = TOKEN_TILE
    n_tiles = t // tm
    cur_blk = lambda i: (jnp.minimum(i, n_tiles - 1), 0)
    prev_blk = lambda i: (jnp.maximum(i - 1, 0), 0)
    return pl.pallas_call(
        functools.partial(_front_kernel, seq_len // tm),
        grid=(n_tiles + 1,),
        in_specs=[
            pl.BlockSpec((tm, D_MODEL), cur_blk),
            _resident((1, D_MODEL)),
            _resident((D_MODEL, NZ)),
            _resident((D_MODEL, LANE)),
            _resident((LANE, GLA_W)), _resident((LANE, GLA_W)), _resident((1, GLA_W)), _resident((1, GLA_W)),
            pl.BlockSpec(memory_space=pltpu.SMEM),
        ],
        out_specs=[
            pl.BlockSpec((tm, ZC), cur_blk),
            pl.BlockSpec((tm, GLA_W), prev_blk),
            pl.BlockSpec((tm, SWA_QW), prev_blk),
        ],
        out_shape=[
            jax.ShapeDtypeStruct((t, ZC), BF16),
            jax.ShapeDtypeStruct((t, GLA_W), BF16),
            jax.ShapeDtypeStruct((t, SWA_QW), BF16),
        ],
        scratch_shapes=[pltpu.VMEM((2, tm, ZM), BF16),
                        pltpu.VMEM((2, tm, LANE), F32),
                        pltpu.VMEM((SWA_WINDOW, 2 * SWA_KW), BF16),
                        pltpu.VMEM((GLA_HEADS, GLA_DV, GLA_DK), F32)],
        compiler_params=_params(1),
        name="front",
    )(h, g, w_main, w_ga, wa_hi, wa_lo, b_alpha, gain, sinks)


def _causal_conv3(y, prev, w):
    n = y.shape[0]
    ext = jnp.concatenate([prev, y], axis=0)
    y1 = ext[SUBLANE - 1:SUBLANE - 1 + n, :]
    y2 = ext[SUBLANE - 2:SUBLANE - 2 + n, :]
    return w[0:1, :] * y2 + w[1:2, :] * y1 + w[2:3, :] * y


def _merge_kernel(tiles_per_seq, h_ref, og_ref, os_ref, cx_ref, cb_ref, cc_ref, cxp_ref, ccp_ref,
                  g1_ref, g2_ref, g3_ref, cw_ref, wg_ref, wc_ref, ws_ref, wo_ref, o_ref):
    seq_start = pl.program_id(0) % tiles_per_seq == 0
    y_gla = _dot(og_ref[...], wg_ref[...])
    y_swa = _dot(os_ref[...], ws_ref[...])
    prod = cc_ref[...].astype(F32) * cx_ref[...].astype(F32)
    prev = cxp_ref[...].astype(F32) * ccp_ref[...].astype(F32)
    prev = jnp.where(seq_start, 0.0, prev)
    conv = cb_ref[...].astype(F32) * _causal_conv3(prod, prev, cw_ref[...])
    y_conv = _dot(conv.astype(BF16), wc_ref[...])

    def gate(g_ref, y):
        return y + jnp.tanh((g_ref[...] * 0.5).astype(F32)) * y

    merged = 0.5 * (gate(g1_ref, y_gla) + gate(g2_ref, y_conv) + gate(g3_ref, y_swa))
    o_ref[...] = h_ref[...] + _dot(merged.astype(BF16), wo_ref[...])


def _merge(h, o_gla, o_swa, zc, conv_w, w_gla_o, w_conv_o, w_swa_o, w_o, seq_len):
    t = h.shape[0]
    tm = TOKEN_TILE
    zblk = lambda width, off: pl.BlockSpec((tm, width), lambda i: (i, off // width))
    halo = lambda off: pl.BlockSpec(
        (SUBLANE, CONV_CH), lambda i: (jnp.maximum(i * (tm // SUBLANE) - 1, 0), off // CONV_CH))
    return pl.pallas_call(
        functools.partial(_merge_kernel, seq_len // tm),
        grid=(t // tm,),
        in_specs=[
            pl.BlockSpec((tm, D_MODEL), lambda i: (i, 0)),
            pl.BlockSpec((tm, GLA_W), lambda i: (i, 0)),
            pl.BlockSpec((tm, SWA_QW), lambda i: (i, 0)),
            zblk(CONV_CH, Z_CX), zblk(CONV_CH, Z_CB), zblk(CONV_CH, Z_CC),
            halo(Z_CX), halo(Z_CC),
            zblk(D_MODEL, Z_GATES), zblk(D_MODEL, Z_GATES + D_MODEL), zblk(D_MODEL, Z_GATES + 2 * D_MODEL),
            _resident((3, CONV_CH)),
            _resident((GLA_W, D_MODEL)),
            _resident((CONV_CH, D_MODEL)),
            _resident((SWA_QW, D_MODEL)),
            _resident((D_MODEL, D_MODEL)),
        ],
        out_specs=pl.BlockSpec((tm, D_MODEL), lambda i: (i, 0)),
        out_shape=jax.ShapeDtypeStruct((t, D_MODEL), F32),
        compiler_params=_params(1),
        name="merge",
    )(h, o_gla, o_swa, zc, zc, zc, zc, zc, zc, zc, zc, conv_w, w_gla_o, w_conv_o, w_swa_o, w_o)


def _ffn_kernel(tiles_per_seq, final_norm, h_ref, g_ref, wup_ref, cw_ref, wd_ref, gf_ref, o_ref,
                act_ref, tail_ref):
    @pl.when(pl.program_id(0) % tiles_per_seq == 0)
    def _():
        tail_ref[...] = jnp.zeros_like(tail_ref)

    x = h_ref[...]
    u = _rms(x, g_ref[...]).astype(BF16)
    n = x.shape[0]
    nf = D_FF // FF_TILE
    gcol = lambda c: slice(c * FF_TILE, (c + 1) * FF_TILE)
    vcol = lambda c: slice(D_FF + c * FF_TILE, D_FF + (c + 1) * FF_TILE)

    def up(c):
        return _dot(u, wup_ref[:, gcol(c)]), _dot(u, wup_ref[:, vcol(c)])

    def gated(c, hid):
        hg, hv = hid
        pg = tail_ref[0, :, gcol(c)]
        pv = tail_ref[1, :, gcol(c)]
        tail_ref[0, :, gcol(c)] = hg[n - SUBLANE:, :]
        tail_ref[1, :, gcol(c)] = hv[n - SUBLANE:, :]
        s = _causal_conv3(hg, pg, 0.5 * cw_ref[:, gcol(c)])
        v = _causal_conv3(hv, pv, cw_ref[:, vcol(c)])
        act_ref[:, gcol(c)] = ((s + s * jnp.tanh(s)) * v).astype(BF16)

    hid = up(0)
    for c in range(nf):
        nxt = up(c + 1) if c + 1 < nf else None
        gated(c, hid)
        hid = nxt
    out = x + _dot(act_ref[...], wd_ref[...])
    if final_norm:
        out = _rms(out, gf_ref[...])
    o_ref[...] = out


def _ffn(h, g, wup, cw, wd, g_final, seq_len, final_norm):
    t = h.shape[0]
    tm = TOKEN_TILE
    return pl.pallas_call(
        functools.partial(_ffn_kernel, seq_len // tm, final_norm),
        grid=(t // tm,),
        in_specs=[
            pl.BlockSpec((tm, D_MODEL), lambda i: (i, 0)),
            _resident((1, D_MODEL)),
            _resident((D_MODEL, 2 * D_FF)),
            _resident((3, 2 * D_FF)),
            _resident((D_FF, D_MODEL)),
            _resident((1, D_MODEL)),
        ],
        out_specs=pl.BlockSpec((tm, D_MODEL), lambda i: (i, 0)),
        out_shape=jax.ShapeDtypeStruct((t, D_MODEL), F32),
        scratch_shapes=[pltpu.VMEM((tm, D_FF), BF16),
                        pltpu.VMEM((2, SUBLANE, D_FF), F32)],
        compiler_params=_params(1),
        name="ffn",
    )(h, g, wup, cw, wd, g_final)


def _prep_layer(w_in, w_alpha):
    w_main = jnp.concatenate(
        [w_in[:, _SRC_GATES:_SRC_END], w_in[:, _SRC_CONV:_SRC_SWA],
         w_in[:, _SRC_GLA:_SRC_GA], w_in[:, _SRC_SWA:_SRC_GATES]], axis=1).astype(BF16)
    w_ga = jnp.pad(w_in[:, _SRC_GA:_SRC_CONV], ((0, 0), (0, LANE - GLA_RANK))).astype(BF16)
    wa = jnp.pad(w_alpha, ((0, LANE - GLA_RANK), (0, 0)))
    wa_hi = wa.astype(BF16)
    wa_lo = (wa - wa_hi.astype(F32)).astype(BF16)
    return w_main, w_ga, wa_hi, wa_lo


def kernel(x, g_mix, w_in, gla_w_alpha, gla_b_alpha, gla_norm_g, conv_w, swa_sinks, w_gla_o,
           w_conv_o, w_swa_o, w_o, g_ffn, w_up, ffn_conv_w, w_down, g_final):
    bsz, seq_len, d = x.shape
    assert d == D_MODEL and seq_len % TOKEN_TILE == 0
    depth = w_in.shape[0]
    h = x.reshape(bsz * seq_len, d)
    for l in range(depth):
        w_main, w_ga, wa_hi, wa_lo = _prep_layer(w_in[l], gla_w_alpha[l])
        zc, o_gla, o_swa = _front(h, g_mix[l][None, :], w_main, w_ga, wa_hi, wa_lo, gla_b_alpha[l][None, :],
                                  gla_norm_g[l][None, :], swa_sinks[l], seq_len)
        h = _merge(h, o_gla, o_swa, zc, conv_w[l], w_gla_o[l].astype(BF16), w_conv_o[l].astype(BF16),
                   w_swa_o[l].astype(BF16), w_o[l].astype(BF16), seq_len)
        h = _ffn(h, g_ffn[l][None, :], w_up[l].astype(BF16), ffn_conv_w[l], w_down[l].astype(BF16),
                 g_final[None, :], seq_len, final_norm=(l == depth - 1))
    return h.reshape(bsz, seq_len, d)
```

```python
import functools

import jax
import jax.numpy as jnp
from jax import lax
from jax.experimental import pallas as pl
from jax.experimental.pallas import tpu as pltpu

F32 = jnp.float32
BF16 = jnp.bfloat16

D_MODEL = 1024
GLA_HEADS = 4
GLA_DK = 128
GLA_DV = 128
GLA_RANK = 16
GLA_TAU = 16.0
GLA_CHUNK = 64
CONV_CH = D_MODEL // 2
SWA_Q_HEADS = 8
SWA_KV_HEADS = 2
SWA_GROUP = SWA_Q_HEADS // SWA_KV_HEADS
SWA_HEAD_DIM = 64
SWA_WINDOW = 128
D_FF = 2816
EPS = 1e-6

GLA_W = GLA_HEADS * GLA_DK
SWA_QW = SWA_Q_HEADS * SWA_HEAD_DIM
SWA_KW = SWA_KV_HEADS * SWA_HEAD_DIM

_SRC_GLA = 0
_SRC_GA = 4 * GLA_W
_SRC_CONV = _SRC_GA + GLA_RANK
_SRC_SWA = _SRC_CONV + 3 * CONV_CH
_SRC_GATES = _SRC_SWA + SWA_QW + 2 * SWA_KW
_SRC_END = _SRC_GATES + 3 * D_MODEL

Z_GATES = 0
N_GATES = 3 * D_MODEL
Z_CC = N_GATES
Z_CX = Z_CC + CONV_CH
Z_CB = Z_CX + CONV_CH
ZC = Z_CB + CONV_CH
M_Q, M_K, M_V, M_R = 0, GLA_W, 2 * GLA_W, 3 * GLA_W
M_SQ = 4 * GLA_W
M_SK = M_SQ + SWA_QW
M_SV = M_SK + SWA_KW
ZM = M_SV + SWA_KW
NZ = ZC + ZM

LANE = 128
SUBLANE = 8
TOKEN_TILE = 512
MERGE_TILE = 1024
FF_TILE = 256
PROJ_CHUNK = 512
VMEM_LIMIT = 56 * 1024 * 1024


def _rms(x, g):
    return x * lax.rsqrt(jnp.mean(x * x, axis=-1, keepdims=True) + EPS) * g


def _dot(a, b):
    return jnp.dot(a, b, preferred_element_type=F32)


def _dot_nt(a, b):
    return lax.dot_general(a, b, (((1,), (1,)), ((), ())), preferred_element_type=F32)


def _dot_tn(a, b):
    return lax.dot_general(a, b, (((0,), (0,)), ((), ())), preferred_element_type=F32)


def _log_sigmoid(x):
    return jnp.minimum(x, 0.0) - jnp.log(1.0 + jnp.exp(-jnp.abs(x)))


def _split3(x):
    hi = x.astype(BF16)
    r1 = x - hi.astype(F32)
    mid = r1.astype(BF16)
    lo = (r1 - mid.astype(F32)).astype(BF16)
    return hi, mid, lo


def _resident(shape, layer=None):
    nd = len(shape)
    if layer is None:
        return pl.BlockSpec(shape, lambda *_: (0,) * nd, pipeline_mode=pl.Buffered(1))
    return pl.BlockSpec((None,) + tuple(shape), lambda *_: (layer,) + (0,) * nd, pipeline_mode=pl.Buffered(1))


def _params(n_axes):
    return pltpu.CompilerParams(
        dimension_semantics=("arbitrary",) * n_axes, vmem_limit_bytes=VMEM_LIMIT)


def _causal_conv3(y, prev, w):
    n = y.shape[0]
    ext = jnp.concatenate([prev, y], axis=0)
    y1 = ext[SUBLANE - 1:SUBLANE - 1 + n, :]
    y2 = ext[SUBLANE - 2:SUBLANE - 2 + n, :]
    return w[0:1, :] * y2 + w[1:2, :] * y1 + w[2:3, :] * y


def _gla_stages(seq_start, q_ref, k_ref, v_ref, r_ref, ga_ref, wa_hi_ref, wa_lo_ref, ba_ref,
                g_ref, o_ref, st_ref):
    c_len = GLA_CHUNK
    n_chunks = o_ref.shape[0] // c_len

    @pl.when(seq_start)
    def _():
        st_ref[...] = jnp.zeros_like(st_ref)

    row = lax.broadcasted_iota(jnp.int32, (c_len, c_len), 0)
    col = lax.broadcasted_iota(jnp.int32, (c_len, c_len), 1)
    causal = row >= col
    tril = jnp.where(causal, 1.0, 0.0).astype(BF16)
    scale = GLA_DK ** -0.5
    gain = g_ref[...]
    heads = range(GLA_HEADS)
    hsl = [slice(h * GLA_DK, (h + 1) * GLA_DK) for h in heads]
    rsl = [slice(c * c_len, (c + 1) * c_len) for c in range(n_chunks)]

    ga = ga_ref[...]
    ga_hi = ga.astype(BF16)
    ga_lo = (ga - ga_hi.astype(F32)).astype(BF16)
    pre = (_dot(ga_hi, wa_hi_ref[...]) + _dot(ga_lo, wa_hi_ref[...])
           + _dot(ga_hi, wa_lo_ref[...]) + ba_ref[...])
    yield
    la = _log_sigmoid(pre) * (1.0 / GLA_TAU)
    la_c = jnp.concatenate([la[rs, :] for rs in rsl], axis=1)
    la_hi, la_mid, la_lo = _split3(la_c)
    yield
    b_c = _dot(tril, la_hi) + _dot(tril, la_mid) + _dot(tril, la_lo)
    yield

    def scaled(c):
        b = b_c[:, c * GLA_W:(c + 1) * GLA_W]
        b_mid = b[c_len // 2 - 1:c_len // 2, :]
        b_last = b[c_len - 1:c_len, :]
        e_q = jnp.exp(b - b_mid)
        q_in = q_ref[rsl[c], :].astype(F32) * scale * e_q
        k_in = k_ref[rsl[c], :].astype(F32) / e_q
        return dict(
            q_in=q_in.astype(BF16),
            k_in=k_in.astype(BF16),
            q_ex=(q_in * jnp.exp(b_mid)).astype(BF16),
            k_ex=(k_in * jnp.exp(b_last - b_mid)).astype(BF16),
            decay=jnp.exp(b_last),
            v=v_ref[rsl[c], :])

    def scores(s):
        s["attn"] = [_dot_nt(s["q_in"][:, hs], s["k_in"][:, hs]) for hs in hsl]
        s["kv"] = [_dot_tn(s["v"][:, hs], s["k_ex"][:, hs]) for hs in hsl]

    def masked(s):
        s["attn"] = [jnp.where(causal, a, 0.0).astype(BF16) for a in s["attn"]]

    def intra(s):
        s["o"] = [_dot(a, s["v"][:, hs]) for a, hs in zip(s["attn"], hsl)]

    quarters = [list(range(q * n_chunks // 4, (q + 1) * n_chunks // 4)) for q in range(4)]
    parts = {}
    for qt in quarters:
        for c in qt:
            parts[c] = scaled(c)
        yield
        for c in qt:
            scores(parts[c])
    yield
    for qt in quarters:
        for c in qt:
            masked(parts[c])
        yield
        for c in qt:
            intra(parts[c])
    yield

    states = [st_ref[h] for h in heads]
    for c in range(n_chunks):
        s = parts[c]
        inter = [_dot_nt(s["q_ex"][:, hs], states[h].astype(BF16)) for h, hs in enumerate(hsl)]
        states = [states[h] * s["decay"][:, hs] + s["kv"][h] for h, hs in enumerate(hsl)]
        if c % 2 == 0:
            yield
        for h, hs in enumerate(hsl):
            o = s["o"][h] + inter[h]
            o = o * lax.rsqrt(jnp.mean(o * o, axis=-1, keepdims=True) + EPS) * gain[:, hs]
            hr = 0.5 * r_ref[rsl[c], hs].astype(F32)
            o_ref[rsl[c], hs] = (o * (hr + hr * jnp.tanh(hr))).astype(BF16)
    for h in heads:
        st_ref[h] = states[h]


def _swa_stages(seq_start, sink_ref, q_ref, k_ref, v_ref, kp, vp, o_ref):
    w = SWA_WINDOW
    hd = SWA_HEAD_DIM
    grp = SWA_GROUP
    n_blk = q_ref.shape[0] // w
    scale = hd ** -0.5

    row = lax.broadcasted_iota(jnp.int32, (grp * w, 2 * w), 0)
    col = lax.broadcasted_iota(jnp.int32, (grp * w, 2 * w), 1)
    head_in_group = row // w
    dist = w + (row % w) - col
    valid = jnp.logical_and(dist >= 0, dist < w)
    no_prev = jnp.logical_and(seq_start, col < w)
    grp_col = lax.broadcasted_iota(jnp.int32, (grp * w, 1), 0) // w

    bias, bias_first, sinks, keys, vals = [], [], [], [], []
    for h in range(SWA_KV_HEADS):
        slope = jnp.zeros((grp * w, 2 * w), F32)
        sink = jnp.zeros((grp * w, 1), F32)
        for g in range(grp):
            hq = h * grp + g
            slope = jnp.where(head_in_group == g, 2.0 ** (-(8.0 / SWA_Q_HEADS) * (hq + 1)), slope)
            sink = jnp.where(grp_col == g, sink_ref[hq], sink)
        b = jnp.where(valid, -slope * dist.astype(F32), -jnp.inf)
        bias.append(b)
        bias_first.append(jnp.where(no_prev, -jnp.inf, b))
        sinks.append(sink)
        ks = slice(h * hd, (h + 1) * hd)
        keys.append(jnp.concatenate([kp[:, ks], k_ref[:, ks]], axis=0))
        v_h = jnp.concatenate([vp[:, ks], v_ref[:, ks]], axis=0)
        vals.append(jnp.concatenate([v_h, jnp.ones_like(v_h)], axis=1))
    yield

    def scores(h, j):
        qs = [q_ref[j * w:(j + 1) * w, (h * grp + g) * hd:(h * grp + g + 1) * hd] for g in range(grp)]
        q = (jnp.concatenate(qs, axis=0).astype(F32) * scale).astype(BF16)
        return _dot_nt(q, keys[h][j * w:(j + 2) * w, :])

    def finish(h, j, s):
        logits = s + (bias_first[h] if j == 0 else bias[h])
        m = jnp.maximum(jnp.max(logits, axis=-1, keepdims=True), sinks[h])
        p = jnp.exp(logits - m).astype(BF16)
        ov = _dot(p, vals[h][j * w:(j + 2) * w, :])
        o = ov[:, :hd] / (ov[:, hd:] + jnp.exp(sinks[h] - m))
        for g in range(grp):
            hq = h * grp + g
            o_ref[j * w:(j + 1) * w, hq * hd:(hq + 1) * hd] = o[g * w:(g + 1) * w, :].astype(BF16)

    jobs = [(h, j) for h in range(SWA_KV_HEADS) for j in range(n_blk)]
    s_next = scores(*jobs[0])
    for idx, (h, j) in enumerate(jobs):
        s = s_next
        if idx + 1 < len(jobs):
            s_next = scores(*jobs[idx + 1])
        yield
        finish(h, j, s)


def _front_kernel(tiles_per_seq, n_tiles, layer, h_ref, g_ref, w_ref, wga_ref, wa_hi_ref, wa_lo_ref, ba_ref,
                  gn_ref, cw_ref, sink_ref, tg_ref, cv_ref, og_ref, os_ref,
                  zm_ref, ga_scr, kvt_ref, st_ref, ptail_ref):
    i = pl.program_id(0)
    cur = i % 2
    prv = 1 - cur
    n = h_ref.shape[0]
    tile = jnp.minimum(i, n_tiles - 1)

    @pl.when(i == 0)
    def _():
        zm_ref[1] = jnp.zeros(zm_ref.shape[1:], zm_ref.dtype)
        ga_scr[1] = jnp.zeros(ga_scr.shape[1:], ga_scr.dtype)
        kvt_ref[...] = jnp.zeros_like(kvt_ref)
        ptail_ref[...] = jnp.zeros_like(ptail_ref)

    u = _rms(h_ref[...], g_ref[...]).astype(BF16)
    seq_start = jnp.maximum(i - 1, 0) % tiles_per_seq == 0
    zprev = lambda off, width: zm_ref.at[prv, :, off:off + width]
    kv_tail = kvt_ref[...]
    mixers = [
        _gla_stages(seq_start, zprev(M_Q, GLA_W), zprev(M_K, GLA_W), zprev(M_V, GLA_W), zprev(M_R, GLA_W),
                    ga_scr.at[prv], wa_hi_ref, wa_lo_ref, ba_ref, gn_ref, og_ref, st_ref),
        _swa_stages(seq_start, sink_ref.at[layer], zprev(M_SQ, SWA_QW), zprev(M_SK, SWA_KW), zprev(M_SV, SWA_KW),
                    kv_tail[:, :SWA_KW], kv_tail[:, SWA_KW:], os_ref),
    ]

    def advance():
        for gen in list(mixers):
            try:
                next(gen)
            except StopIteration:
                mixers.remove(gen)

    held = {}
    for off in range(0, NZ, PROJ_CHUNK):
        wd = min(PROJ_CHUNK, NZ - off)
        res = _dot(u, w_ref[:, off:off + wd])
        if off < N_GATES:
            tg_ref[:, off:off + wd] = jnp.tanh(0.5 * res).astype(BF16)
        elif off == Z_CC:
            held["cc"] = res
        elif off == Z_CX:
            prod = held.pop("cc") * res
            prev = jnp.where(tile % tiles_per_seq == 0, 0.0, ptail_ref[(tile + 1) % 2])
            ptail_ref[tile % 2] = prod[n - SUBLANE:, :]
            held["conv"] = _causal_conv3(prod, prev, cw_ref[...])
        elif off == Z_CB:
            cv_ref[...] = (res * held.pop("conv")).astype(BF16)
        else:
            zm_ref[cur, :, off - ZC:off - ZC + wd] = res.astype(BF16)
        advance()
    ga_scr[cur] = _dot(u, wga_ref[...])
    while mixers:
        advance()
    kvt_ref[...] = zm_ref[prv, n - SWA_WINDOW:, M_SK:M_SK + 2 * SWA_KW]


def _front(layer, h, g, w_main, w_ga, wa_hi, wa_lo, b_alpha, gain, conv_w, sinks, seq_len):
    t = h.shape[0]
    tm = TOKEN_TILE
    n_tiles = t // tm
    cur_blk = lambda i: (jnp.minimum(i, n_tiles - 1), 0)
    prev_blk = lambda i: (jnp.maximum(i - 1, 0), 0)
    return pl.pallas_call(
        functools.partial(_front_kernel, seq_len // tm, n_tiles, layer),
        grid=(n_tiles + 1,),
        in_specs=[
            pl.BlockSpec((tm, D_MODEL), cur_blk),
            _resident((1, D_MODEL), layer),
            _resident((D_MODEL, NZ), layer),
            _resident((D_MODEL, LANE), layer),
            _resident((LANE, GLA_W), layer), _resident((LANE, GLA_W), layer),
            _resident((1, GLA_W), layer), _resident((1, GLA_W), layer),
            _resident((3, CONV_CH), layer),
            pl.BlockSpec(memory_space=pltpu.SMEM),
        ],
        out_specs=[
            pl.BlockSpec((tm, N_GATES), cur_blk),
            pl.BlockSpec((tm, CONV_CH), cur_blk),
            pl.BlockSpec((tm, GLA_W), prev_blk),
            pl.BlockSpec((tm, SWA_QW), prev_blk),
        ],
        out_shape=[
            jax.ShapeDtypeStruct((t, N_GATES), BF16),
            jax.ShapeDtypeStruct((t, CONV_CH), BF16),
            jax.ShapeDtypeStruct((t, GLA_W), BF16),
            jax.ShapeDtypeStruct((t, SWA_QW), BF16),
        ],
        scratch_shapes=[pltpu.VMEM((2, tm, ZM), BF16),
                        pltpu.VMEM((2, tm, LANE), F32),
                        pltpu.VMEM((SWA_WINDOW, 2 * SWA_KW), BF16),
                        pltpu.VMEM((GLA_HEADS, GLA_DV, GLA_DK), F32),
                        pltpu.VMEM((2, SUBLANE, CONV_CH), F32)],
        compiler_params=_params(1),
        name="front",
    )(h, g, w_main, w_ga, wa_hi, wa_lo, b_alpha, gain, conv_w, sinks)


def _merge_kernel(h_ref, og_ref, os_ref, cv_ref, t1_ref, t2_ref, t3_ref, gf_ref,
                  wg_ref, wc_ref, ws_ref, wo_ref, o_ref, u_ref):
    y_gla = _dot(og_ref[...], wg_ref[...])
    y_conv = _dot(cv_ref[...], wc_ref[...])
    y_swa = _dot(os_ref[...], ws_ref[...])

    def gate(t_ref, y):
        return y + t_ref[...].astype(F32) * y

    merged = 0.5 * (gate(t1_ref, y_gla) + gate(t2_ref, y_conv) + gate(t3_ref, y_swa))
    x = h_ref[...] + _dot(merged.astype(BF16), wo_ref[...])
    o_ref[...] = x
    u_ref[...] = _rms(x, gf_ref[...]).astype(BF16)


def _merge(layer, h, o_gla, o_swa, cv, tg, g_ffn, w_gla_o, w_conv_o, w_swa_o, w_o):
    t = h.shape[0]
    tm = MERGE_TILE
    row = lambda width, j=0: pl.BlockSpec((tm, width), lambda i: (i, j))
    return pl.pallas_call(
        _merge_kernel,
        grid=(t // tm,),
        in_specs=[
            row(D_MODEL), row(GLA_W), row(SWA_QW), row(CONV_CH),
            row(D_MODEL, 0), row(D_MODEL, 1), row(D_MODEL, 2),
            _resident((1, D_MODEL), layer),
            _resident((GLA_W, D_MODEL), layer),
            _resident((CONV_CH, D_MODEL), layer),
            _resident((SWA_QW, D_MODEL), layer),
            _resident((D_MODEL, D_MODEL), layer),
        ],
        out_specs=[row(D_MODEL), row(D_MODEL)],
        out_shape=[jax.ShapeDtypeStruct((t, D_MODEL), F32), jax.ShapeDtypeStruct((t, D_MODEL), BF16)],
        compiler_params=_params(1),
        name="merge",
    )(h, o_gla, o_swa, cv, tg, tg, tg, g_ffn, w_gla_o, w_conv_o, w_swa_o, w_o)


def _ffn_kernel(tiles_per_seq, final_norm, h_ref, u_ref, wup_ref, cw_ref, wd_ref, gf_ref, o_ref,
                act_ref, tail_ref):
    @pl.when(pl.program_id(0) % tiles_per_seq == 0)
    def _():
        tail_ref[...] = jnp.zeros_like(tail_ref)

    n = h_ref.shape[0]
    nf = D_FF // FF_TILE
    gcol = lambda c: slice(c * FF_TILE, (c + 1) * FF_TILE)
    vcol = lambda c: slice(D_FF + c * FF_TILE, D_FF + (c + 1) * FF_TILE)

    def up(c):
        return _dot(u_ref[...], wup_ref[:, gcol(c)]), _dot(u_ref[...], wup_ref[:, vcol(c)])

    def gated(c, hid):
        hg, hv = hid
        pg = tail_ref[0, :, gcol(c)]
        pv = tail_ref[1, :, gcol(c)]
        tail_ref[0, :, gcol(c)] = hg[n - SUBLANE:, :]
        tail_ref[1, :, gcol(c)] = hv[n - SUBLANE:, :]
        s = _causal_conv3(hg, pg, 0.5 * cw_ref[:, gcol(c)])
        v = _causal_conv3(hv, pv, cw_ref[:, vcol(c)])
        act_ref[:, gcol(c)] = ((s + s * jnp.tanh(s)) * v).astype(BF16)

    hid = up(0)
    for c in range(nf):
        nxt = up(c + 1) if c + 1 < nf else None
        gated(c, hid)
        hid = nxt
    out = h_ref[...] + _dot(act_ref[...], wd_ref[...])
    if final_norm:
        out = _rms(out, gf_ref[...])
    o_ref[...] = out


def _ffn(layer, h, u, wup, cw, wd, g_final, seq_len, final_norm):
    t = h.shape[0]
    tm = TOKEN_TILE
    return pl.pallas_call(
        functools.partial(_ffn_kernel, seq_len // tm, final_norm),
        grid=(t // tm,),
        in_specs=[
            pl.BlockSpec((tm, D_MODEL), lambda i: (i, 0)),
            pl.BlockSpec((tm, D_MODEL), lambda i: (i, 0)),
            _resident((D_MODEL, 2 * D_FF), layer),
            _resident((3, 2 * D_FF), layer),
            _resident((D_FF, D_MODEL), layer),
            _resident((1, D_MODEL)),
        ],
        out_specs=pl.BlockSpec((tm, D_MODEL), lambda i: (i, 0)),
        out_shape=jax.ShapeDtypeStruct((t, D_MODEL), F32),
        scratch_shapes=[pltpu.VMEM((tm, D_FF), BF16),
                        pltpu.VMEM((2, SUBLANE, D_FF), F32)],
        compiler_params=_params(1),
        name="ffn",
    )(h, u, wup, cw, wd, g_final)


def _prep_weights(w_in, w_alpha):
    w_main = jnp.concatenate(
        [w_in[..., _SRC_GATES:_SRC_END],
         w_in[..., _SRC_CONV + 2 * CONV_CH:_SRC_SWA],
         w_in[..., _SRC_CONV:_SRC_CONV + CONV_CH],
         w_in[..., _SRC_CONV + CONV_CH:_SRC_CONV + 2 * CONV_CH],
         w_in[..., _SRC_GLA:_SRC_GA], w_in[..., _SRC_SWA:_SRC_GATES]], axis=-1).astype(BF16)
    w_ga = jnp.pad(w_in[..., _SRC_GA:_SRC_CONV], ((0, 0), (0, 0), (0, LANE - GLA_RANK))).astype(BF16)
    wa = jnp.pad(w_alpha, ((0, 0), (0, LANE - GLA_RANK), (0, 0)))
    wa_hi = wa.astype(BF16)
    wa_lo = (wa - wa_hi.astype(F32)).astype(BF16)
    return w_main, w_ga, wa_hi, wa_lo


def kernel(x, g_mix, w_in, gla_w_alpha, gla_b_alpha, gla_norm_g, conv_w, swa_sinks, w_gla_o,
           w_conv_o, w_swa_o, w_o, g_ffn, w_up, ffn_conv_w, w_down, g_final):
    bsz, seq_len, d = x.shape
    assert d == D_MODEL and seq_len % TOKEN_TILE == 0 and (bsz * seq_len) % MERGE_TILE == 0
    depth = w_in.shape[0]
    w_main, w_ga, wa_hi, wa_lo = _prep_weights(w_in, gla_w_alpha)
    row = lambda p: p[:, None, :]
    g_mix, b_alpha, gain, g_ffn = row(g_mix), row(gla_b_alpha), row(gla_norm_g), row(g_ffn)
    w_gla_o, w_conv_o, w_swa_o, w_o = (w.astype(BF16) for w in (w_gla_o, w_conv_o, w_swa_o, w_o))
    w_up, w_down = w_up.astype(BF16), w_down.astype(BF16)
    h = x.reshape(bsz * seq_len, d)
    for l in range(depth):
        tg, cv, o_gla, o_swa = _front(l, h, g_mix, w_main, w_ga, wa_hi, wa_lo, b_alpha, gain, conv_w,
                                      swa_sinks, seq_len)
        h, u = _merge(l, h, o_gla, o_swa, cv, tg, g_ffn, w_gla_o, w_conv_o, w_swa_o, w_o)
        h = _ffn(l, h, u, w_up, ffn_conv_w, w_down, g_final[None, :], seq_len, final_norm=(l == depth - 1))
    return h.reshape(bsz, seq_len, d)
```

```python
import functools

import jax
import jax.numpy as jnp
from jax import lax
from jax.experimental import pallas as pl
from jax.experimental.pallas import tpu as pltpu

F32 = jnp.float32
BF16 = jnp.bfloat16

D_MODEL = 1024
GLA_HEADS = 4
GLA_DK = 128
GLA_DV = 128
GLA_RANK = 16
GLA_TAU = 16.0
GLA_CHUNK = 64
CONV_CH = D_MODEL // 2
SWA_Q_HEADS = 8
SWA_KV_HEADS = 2
SWA_GROUP = SWA_Q_HEADS // SWA_KV_HEADS
SWA_HEAD_DIM = 64
SWA_WINDOW = 128
D_FF = 2816
EPS = 1e-6

GLA_W = GLA_HEADS * GLA_DK
SWA_QW = SWA_Q_HEADS * SWA_HEAD_DIM
SWA_KW = SWA_KV_HEADS * SWA_HEAD_DIM

_SRC_GLA = 0
_SRC_GA = 4 * GLA_W
_SRC_CONV = _SRC_GA + GLA_RANK
_SRC_SWA = _SRC_CONV + 3 * CONV_CH
_SRC_GATES = _SRC_SWA + SWA_QW + 2 * SWA_KW
_SRC_END = _SRC_GATES + 3 * D_MODEL

Z_GATES = 0
N_GATES = 3 * D_MODEL
Z_CC = N_GATES
Z_CX = Z_CC + CONV_CH
Z_CB = Z_CX + CONV_CH
ZC = Z_CB + CONV_CH
M_Q, M_K, M_V, M_R = 0, GLA_W, 2 * GLA_W, 3 * GLA_W
M_SQ = 4 * GLA_W
M_SK = M_SQ + SWA_QW
M_SV = M_SK + SWA_KW
ZM = M_SV + SWA_KW
NZ = ZC + ZM

LANE = 128
SUBLANE = 8
TOKEN_TILE = 512
MERGE_TILE = 1024
FF_TILE = 256
PROJ_CHUNK = 512
VMEM_LIMIT = 56 * 1024 * 1024


def _rms(x, g):
    return x * lax.rsqrt(jnp.mean(x * x, axis=-1, keepdims=True) + EPS) * g


def _dot(a, b):
    return jnp.dot(a, b, preferred_element_type=F32)


def _dot_nt(a, b):
    return lax.dot_general(a, b, (((1,), (1,)), ((), ())), preferred_element_type=F32)


def _dot_tn(a, b):
    return lax.dot_general(a, b, (((0,), (0,)), ((), ())), preferred_element_type=F32)


def _log_sigmoid(x):
    return jnp.minimum(x, 0.0) - jnp.log(1.0 + jnp.exp(-jnp.abs(x)))


def _split3(x):
    hi = x.astype(BF16)
    r1 = x - hi.astype(F32)
    mid = r1.astype(BF16)
    lo = (r1 - mid.astype(F32)).astype(BF16)
    return hi, mid, lo


def _resident(shape, layer=None):
    nd = len(shape)
    if layer is None:
        return pl.BlockSpec(shape, lambda *_: (0,) * nd, pipeline_mode=pl.Buffered(1))
    return pl.BlockSpec((None,) + tuple(shape), lambda *_: (layer,) + (0,) * nd, pipeline_mode=pl.Buffered(1))


def _params(n_axes):
    return pltpu.CompilerParams(
        dimension_semantics=("arbitrary",) * n_axes, vmem_limit_bytes=VMEM_LIMIT)


def _causal_conv3(y, prev, w):
    n = y.shape[0]
    ext = jnp.concatenate([prev, y], axis=0)
    y1 = ext[SUBLANE - 1:SUBLANE - 1 + n, :]
    y2 = ext[SUBLANE - 2:SUBLANE - 2 + n, :]
    return w[0:1, :] * y2 + w[1:2, :] * y1 + w[2:3, :] * y


def _gla_stages(seq_start, q_ref, k_ref, v_ref, r_ref, ga_ref, wa_hi_ref, wa_lo_ref, ba_ref,
                g_ref, o_ref, st_ref):
    c_len = GLA_CHUNK
    n_chunks = o_ref.shape[0] // c_len

    @pl.when(seq_start)
    def _():
        st_ref[...] = jnp.zeros_like(st_ref)

    row = lax.broadcasted_iota(jnp.int32, (c_len, c_len), 0)
    col = lax.broadcasted_iota(jnp.int32, (c_len, c_len), 1)
    causal = row >= col
    tril = jnp.where(causal, 1.0, 0.0).astype(BF16)
    scale = GLA_DK ** -0.5
    gain = g_ref[...]
    heads = range(GLA_HEADS)
    hsl = [slice(h * GLA_DK, (h + 1) * GLA_DK) for h in heads]
    rsl = [slice(c * c_len, (c + 1) * c_len) for c in range(n_chunks)]

    ga = ga_ref[...]
    ga_hi = ga.astype(BF16)
    ga_lo = (ga - ga_hi.astype(F32)).astype(BF16)
    pre = (_dot(ga_hi, wa_hi_ref[...]) + _dot(ga_lo, wa_hi_ref[...])
           + _dot(ga_hi, wa_lo_ref[...]) + ba_ref[...])
    yield
    la = _log_sigmoid(pre) * (1.0 / GLA_TAU)
    la_c = jnp.concatenate([la[rs, :] for rs in rsl], axis=1)
    la_hi, la_mid, la_lo = _split3(la_c)
    yield
    b_c = _dot(tril, la_hi) + _dot(tril, la_mid) + _dot(tril, la_lo)
    yield

    def scaled(c):
        b = b_c[:, c * GLA_W:(c + 1) * GLA_W]
        b_mid = b[c_len // 2 - 1:c_len // 2, :]
        b_last = b[c_len - 1:c_len, :]
        e_q = jnp.exp(b - b_mid)
        q_in = q_ref[rsl[c], :].astype(F32) * scale * e_q
        k_in = k_ref[rsl[c], :].astype(F32) / e_q
        return dict(
            q_in=q_in.astype(BF16),
            k_in=k_in.astype(BF16),
            q_ex=(q_in * jnp.exp(b_mid)).astype(BF16),
            k_ex=(k_in * jnp.exp(b_last - b_mid)).astype(BF16),
            decay=jnp.exp(b_last),
            v=v_ref[rsl[c], :])

    def scores(s):
        s["attn"] = [_dot_nt(s["q_in"][:, hs], s["k_in"][:, hs]) for hs in hsl]
        s["kv"] = [_dot_tn(s["v"][:, hs], s["k_ex"][:, hs]) for hs in hsl]

    def masked(s):
        s["attn"] = [jnp.where(causal, a, 0.0).astype(BF16) for a in s["attn"]]

    def intra(s):
        s["o"] = [_dot(a, s["v"][:, hs]) for a, hs in zip(s["attn"], hsl)]

    quarters = [list(range(q * n_chunks // 4, (q + 1) * n_chunks // 4)) for q in range(4)]
    parts = {}
    for qt in quarters:
        for c in qt:
            parts[c] = scaled(c)
        yield
        for c in qt:
            scores(parts[c])
    yield
    for qt in quarters:
        for c in qt:
            masked(parts[c])
        yield
        for c in qt:
            intra(parts[c])
    yield

    states = [st_ref[h] for h in heads]
    for c in range(n_chunks):
        s = parts[c]
        inter = [_dot_nt(s["q_ex"][:, hs], states[h].astype(BF16)) for h, hs in enumerate(hsl)]
        states = [states[h] * s["decay"][:, hs] + s["kv"][h] for h, hs in enumerate(hsl)]
        if c % 2 == 0:
            yield
        for h, hs in enumerate(hsl):
            o = s["o"][h] + inter[h]
            o = o * lax.rsqrt(jnp.mean(o * o, axis=-1, keepdims=True) + EPS) * gain[:, hs]
            hr = 0.5 * r_ref[rsl[c], hs].astype(F32)
            o_ref[rsl[c], hs] = (o * (hr + hr * jnp.tanh(hr))).astype(BF16)
    for h in heads:
        st_ref[h] = states[h]


def _swa_stages(seq_start, sink_ref, q_ref, k_ref, v_ref, kp, vp, o_ref):
    w = SWA_WINDOW
    hd = SWA_HEAD_DIM
    grp = SWA_GROUP
    n_blk = q_ref.shape[0] // w
    scale = hd ** -0.5

    row = lax.broadcasted_iota(jnp.int32, (grp * w, 2 * w), 0)
    col = lax.broadcasted_iota(jnp.int32, (grp * w, 2 * w), 1)
    head_in_group = row // w
    dist = w + (row % w) - col
    valid = jnp.logical_and(dist >= 0, dist < w)
    no_prev = jnp.logical_and(seq_start, col < w)
    grp_col = lax.broadcasted_iota(jnp.int32, (grp * w, 1), 0) // w

    bias, bias_first, sinks, keys, vals = [], [], [], [], []
    for h in range(SWA_KV_HEADS):
        slope = jnp.zeros((grp * w, 2 * w), F32)
        sink = jnp.zeros((grp * w, 1), F32)
        for g in range(grp):
            hq = h * grp + g
            slope = jnp.where(head_in_group == g, 2.0 ** (-(8.0 / SWA_Q_HEADS) * (hq + 1)), slope)
            sink = jnp.where(grp_col == g, sink_ref[hq], sink)
        b = jnp.where(valid, -slope * dist.astype(F32), -jnp.inf)
        bias.append(b)
        bias_first.append(jnp.where(no_prev, -jnp.inf, b))
        sinks.append(sink)
        ks = slice(h * hd, (h + 1) * hd)
        keys.append(jnp.concatenate([kp[:, ks], k_ref[:, ks]], axis=0))
        v_h = jnp.concatenate([vp[:, ks], v_ref[:, ks]], axis=0)
        vals.append(jnp.concatenate([v_h, jnp.ones_like(v_h)], axis=1))
    yield

    def scores(h, j):
        qs = [q_ref[j * w:(j + 1) * w, (h * grp + g) * hd:(h * grp + g + 1) * hd] for g in range(grp)]
        q = (jnp.concatenate(qs, axis=0).astype(F32) * scale).astype(BF16)
        return _dot_nt(q, keys[h][j * w:(j + 2) * w, :])

    def finish(h, j, s):
        logits = s + (bias_first[h] if j == 0 else bias[h])
        m = jnp.maximum(jnp.max(logits, axis=-1, keepdims=True), sinks[h])
        p = jnp.exp(logits - m).astype(BF16)
        ov = _dot(p, vals[h][j * w:(j + 2) * w, :])
        o = ov[:, :hd] / (ov[:, hd:] + jnp.exp(sinks[h] - m))
        for g in range(grp):
            hq = h * grp + g
            o_ref[j * w:(j + 1) * w, hq * hd:(hq + 1) * hd] = o[g * w:(g + 1) * w, :].astype(BF16)

    jobs = [(h, j) for h in range(SWA_KV_HEADS) for j in range(n_blk)]
    s_next = scores(*jobs[0])
    for idx, (h, j) in enumerate(jobs):
        s = s_next
        if idx + 1 < len(jobs):
            s_next = scores(*jobs[idx + 1])
        yield
        finish(h, j, s)


def _front_kernel(tiles_per_seq, n_tiles, layer, h_ref, g_ref, w_ref, wga_ref, wa_hi_ref, wa_lo_ref, ba_ref,
                  gn_ref, cw_ref, sink_ref, tg_ref, cv_ref, og_ref, os_ref,
                  zm_ref, ga_scr, kvt_ref, st_ref, ptail_ref):
    i = pl.program_id(0)
    cur = i % 2
    prv = 1 - cur
    n = h_ref.shape[0]
    tile = jnp.minimum(i, n_tiles - 1)

    @pl.when(i == 0)
    def _():
        zm_ref[1] = jnp.zeros(zm_ref.shape[1:], zm_ref.dtype)
        ga_scr[1] = jnp.zeros(ga_scr.shape[1:], ga_scr.dtype)
        kvt_ref[...] = jnp.zeros_like(kvt_ref)
        ptail_ref[...] = jnp.zeros_like(ptail_ref)

    u = _rms(h_ref[...], g_ref[...]).astype(BF16)
    seq_start = jnp.maximum(i - 1, 0) % tiles_per_seq == 0
    zprev = lambda off, width: zm_ref.at[prv, :, off:off + width]
    kv_tail = kvt_ref[...]
    mixers = [
        _gla_stages(seq_start, zprev(M_Q, GLA_W), zprev(M_K, GLA_W), zprev(M_V, GLA_W), zprev(M_R, GLA_W),
                    ga_scr.at[prv], wa_hi_ref, wa_lo_ref, ba_ref, gn_ref, og_ref, st_ref),
        _swa_stages(seq_start, sink_ref.at[layer], zprev(M_SQ, SWA_QW), zprev(M_SK, SWA_KW), zprev(M_SV, SWA_KW),
                    kv_tail[:, :SWA_KW], kv_tail[:, SWA_KW:], os_ref),
    ]

    def advance():
        for gen in list(mixers):
            try:
                next(gen)
            except StopIteration:
                mixers.remove(gen)

    held = {}
    for off in range(0, NZ, PROJ_CHUNK):
        wd = min(PROJ_CHUNK, NZ - off)
        res = _dot(u, w_ref[:, off:off + wd])
        if off < N_GATES:
            tg_ref[:, off:off + wd] = jnp.tanh(0.5 * res).astype(BF16)
        elif off == Z_CC:
            held["cc"] = res
        elif off == Z_CX:
            prod = held.pop("cc") * res
            prev = jnp.where(tile % tiles_per_seq == 0, 0.0, ptail_ref[(tile + 1) % 2])
            ptail_ref[tile % 2] = prod[n - SUBLANE:, :]
            held["conv"] = _causal_conv3(prod, prev, cw_ref[...])
        elif off == Z_CB:
            cv_ref[...] = (res * held.pop("conv")).astype(BF16)
        else:
            zm_ref[cur, :, off - ZC:off - ZC + wd] = res.astype(BF16)
        advance()
    ga_scr[cur] = _dot(u, wga_ref[...])
    while mixers:
        advance()
    kvt_ref[...] = zm_ref[prv, n - SWA_WINDOW:, M_SK:M_SK + 2 * SWA_KW]


def _front(layer, h, g, w_main, w_ga, wa_hi, wa_lo, b_alpha, gain, conv_w, sinks, seq_len):
    t = h.shape[0]
    tm = TOKEN_TILE
    n_tiles = t // tm
    cur_blk = lambda i: (jnp.minimum(i, n_tiles - 1), 0)
    prev_blk = lambda i: (jnp.maximum(i - 1, 0), 0)
    return pl.pallas_call(
        functools.partial(_front_kernel, seq_len // tm, n_tiles, layer),
        grid=(n_tiles + 1,),
        in_specs=[
            pl.BlockSpec((tm, D_MODEL), cur_blk),
            _resident((1, D_MODEL), layer),
            _resident((D_MODEL, NZ), layer),
            _resident((D_MODEL, LANE), layer),
            _resident((LANE, GLA_W), layer), _resident((LANE, GLA_W), layer),
            _resident((1, GLA_W), layer), _resident((1, GLA_W), layer),
            _resident((3, CONV_CH), layer),
            pl.BlockSpec(memory_space=pltpu.SMEM),
        ],
        out_specs=[
            pl.BlockSpec((tm, N_GATES), cur_blk),
            pl.BlockSpec((tm, CONV_CH), cur_blk),
            pl.BlockSpec((tm, GLA_W), prev_blk),
            pl.BlockSpec((tm, SWA_QW), prev_blk),
        ],
        out_shape=[
            jax.ShapeDtypeStruct((t, N_GATES), BF16),
            jax.ShapeDtypeStruct((t, CONV_CH), BF16),
            jax.ShapeDtypeStruct((t, GLA_W), BF16),
            jax.ShapeDtypeStruct((t, SWA_QW), BF16),
        ],
        scratch_shapes=[pltpu.VMEM((2, tm, ZM), BF16),
                        pltpu.VMEM((2, tm, LANE), F32),
                        pltpu.VMEM((SWA_WINDOW, 2 * SWA_KW), BF16),
                        pltpu.VMEM((GLA_HEADS, GLA_DV, GLA_DK), F32),
                        pltpu.VMEM((2, SUBLANE, CONV_CH), F32)],
        compiler_params=_params(1),
        name="front",
    )(h, g, w_main, w_ga, wa_hi, wa_lo, b_alpha, gain, conv_w, sinks)


def _merge_kernel(h_ref, og_ref, os_ref, cv_ref, t1_ref, t2_ref, t3_ref,
                  wg_ref, wc_ref, ws_ref, wo_ref, o_ref):
    y_gla = _dot(og_ref[...], wg_ref[...])
    y_conv = _dot(cv_ref[...], wc_ref[...])
    y_swa = _dot(os_ref[...], ws_ref[...])

    def gate(t_ref, y):
        return y + t_ref[...].astype(F32) * y

    merged = 0.5 * (gate(t1_ref, y_gla) + gate(t2_ref, y_conv) + gate(t3_ref, y_swa))
    o_ref[...] = h_ref[...] + _dot(merged.astype(BF16), wo_ref[...])


def _merge(layer, h, o_gla, o_swa, cv, tg, w_gla_o, w_conv_o, w_swa_o, w_o):
    t = h.shape[0]
    tm = MERGE_TILE
    row = lambda width, j=0: pl.BlockSpec((tm, width), lambda i: (i, j))
    return pl.pallas_call(
        _merge_kernel,
        grid=(t // tm,),
        in_specs=[
            row(D_MODEL), row(GLA_W), row(SWA_QW), row(CONV_CH),
            row(D_MODEL, 0), row(D_MODEL, 1), row(D_MODEL, 2),
            _resident((GLA_W, D_MODEL), layer),
            _resident((CONV_CH, D_MODEL), layer),
            _resident((SWA_QW, D_MODEL), layer),
            _resident((D_MODEL, D_MODEL), layer),
        ],
        out_specs=row(D_MODEL),
        out_shape=jax.ShapeDtypeStruct((t, D_MODEL), F32),
        compiler_params=_params(1),
        name="merge",
    )(h, o_gla, o_swa, cv, tg, tg, tg, w_gla_o, w_conv_o, w_swa_o, w_o)


def _ffn_kernel(tiles_per_seq, final_norm, h_ref, g_ref, wup_ref, cw_ref, wd_ref, gf_ref, o_ref,
                act_ref, tail_ref):
    @pl.when(pl.program_id(0) % tiles_per_seq == 0)
    def _():
        tail_ref[...] = jnp.zeros_like(tail_ref)

    x = h_ref[...]
    u = _rms(x, g_ref[...]).astype(BF16)
    n = x.shape[0]
    nf = D_FF // FF_TILE
    gcol = lambda c: slice(c * FF_TILE, (c + 1) * FF_TILE)
    vcol = lambda c: slice(D_FF + c * FF_TILE, D_FF + (c + 1) * FF_TILE)

    def up(c):
        return _dot(u, wup_ref[:, gcol(c)]), _dot(u, wup_ref[:, vcol(c)])

    def gated(c, hid):
        hg, hv = hid
        pg = tail_ref[0, :, gcol(c)]
        pv = tail_ref[1, :, gcol(c)]
        tail_ref[0, :, gcol(c)] = hg[n - SUBLANE:, :]
        tail_ref[1, :, gcol(c)] = hv[n - SUBLANE:, :]
        s = _causal_conv3(hg, pg, 0.5 * cw_ref[:, gcol(c)])
        v = _causal_conv3(hv, pv, cw_ref[:, vcol(c)])
        act_ref[:, gcol(c)] = ((s + s * jnp.tanh(s)) * v).astype(BF16)

    hid = up(0)
    for c in range(nf):
        nxt = up(c + 1) if c + 1 < nf else None
        gated(c, hid)
        hid = nxt
    out = x + _dot(act_ref[...], wd_ref[...])
    if final_norm:
        out = _rms(out, gf_ref[...])
    o_ref[...] = out


def _ffn(layer, h, g, wup, cw, wd, g_final, seq_len, final_norm):
    t = h.shape[0]
    tm = TOKEN_TILE
    return pl.pallas_call(
        functools.partial(_ffn_kernel, seq_len // tm, final_norm),
        grid=(t // tm,),
        in_specs=[
            pl.BlockSpec((tm, D_MODEL), lambda i: (i, 0)),
            _resident((1, D_MODEL), layer),
            _resident((D_MODEL, 2 * D_FF), layer),
            _resident((3, 2 * D_FF), layer),
            _resident((D_FF, D_MODEL), layer),
            _resident((1, D_MODEL)),
        ],
        out_specs=pl.BlockSpec((tm, D_MODEL), lambda i: (i, 0)),
        out_shape=jax.ShapeDtypeStruct((t, D_MODEL), F32),
        scratch_shapes=[pltpu.VMEM((tm, D_FF), BF16),
                        pltpu.VMEM((2, SUBLANE, D_FF), F32)],
        compiler_params=_params(1),
        name="ffn",
    )(h, g, wup, cw, wd, g_final)


def _prep_weights(w_in, w_alpha):
    w_in = w_in.astype(BF16)
    w_main = jnp.concatenate(
        [w_in[..., _SRC_GATES:_SRC_END],
         w_in[..., _SRC_CONV + 2 * CONV_CH:_SRC_SWA],
         w_in[..., _SRC_CONV:_SRC_CONV + CONV_CH],
         w_in[..., _SRC_CONV + CONV_CH:_SRC_CONV + 2 * CONV_CH],
         w_in[..., _SRC_GLA:_SRC_GA], w_in[..., _SRC_SWA:_SRC_GATES]], axis=-1)
    w_ga = jnp.pad(w_in[..., _SRC_GA:_SRC_CONV], ((0, 0), (0, 0), (0, LANE - GLA_RANK)))
    wa = jnp.pad(w_alpha, ((0, 0), (0, LANE - GLA_RANK), (0, 0)))
    wa_hi = wa.astype(BF16)
    wa_lo = (wa - wa_hi.astype(F32)).astype(BF16)
    return w_main, w_ga, wa_hi, wa_lo


def kernel(x, g_mix, w_in, gla_w_alpha, gla_b_alpha, gla_norm_g, conv_w, swa_sinks, w_gla_o,
           w_conv_o, w_swa_o, w_o, g_ffn, w_up, ffn_conv_w, w_down, g_final):
    bsz, seq_len, d = x.shape
    assert d == D_MODEL and seq_len % TOKEN_TILE == 0 and (bsz * seq_len) % MERGE_TILE == 0
    depth = w_in.shape[0]
    w_main, w_ga, wa_hi, wa_lo = _prep_weights(w_in, gla_w_alpha)
    row = lambda p: p[:, None, :]
    g_mix, b_alpha, gain, g_ffn = row(g_mix), row(gla_b_alpha), row(gla_norm_g), row(g_ffn)
    w_gla_o, w_conv_o, w_swa_o, w_o = (w.astype(BF16) for w in (w_gla_o, w_conv_o, w_swa_o, w_o))
    w_up, w_down = w_up.astype(BF16), w_down.astype(BF16)
    h = x.reshape(bsz * seq_len, d)
    for l in range(depth):
        tg, cv, o_gla, o_swa = _front(l, h, g_mix, w_main, w_ga, wa_hi, wa_lo, b_alpha, gain, conv_w,
                                      swa_sinks, seq_len)
        h = _merge(l, h, o_gla, o_swa, cv, tg, w_gla_o, w_conv_o, w_swa_o, w_o)
        h = _ffn(l, h, g_ffn, w_up, ffn_conv_w, w_down, g_final[None, :], seq_len, final_norm=(l == depth - 1))
    return h.reshape(bsz, seq_len, d)
```

```python
import functools

import jax
import jax.numpy as jnp
from jax import lax
from jax.experimental import pallas as pl
from jax.experimental.pallas import tpu as pltpu

F32 = jnp.float32
BF16 = jnp.bfloat16

D_MODEL = 1024
GLA_HEADS = 4
GLA_DK = 128
GLA_DV = 128
GLA_RANK = 16
GLA_TAU = 16.0
GLA_CHUNK = 64
CONV_CH = D_MODEL // 2
SWA_Q_HEADS = 8
SWA_KV_HEADS = 2
SWA_GROUP = SWA_Q_HEADS // SWA_KV_HEADS
SWA_HEAD_DIM = 64
SWA_WINDOW = 128
D_FF = 2816
EPS = 1e-6

GLA_W = GLA_HEADS * GLA_DK
SWA_QW = SWA_Q_HEADS * SWA_HEAD_DIM
SWA_KW = SWA_KV_HEADS * SWA_HEAD_DIM

_SRC_GLA = 0
_SRC_GA = 4 * GLA_W
_SRC_CONV = _SRC_GA + GLA_RANK
_SRC_SWA = _SRC_CONV + 3 * CONV_CH
_SRC_GATES = _SRC_SWA + SWA_QW + 2 * SWA_KW
_SRC_END = _SRC_GATES + 3 * D_MODEL

Z_GATES = 0
N_GATES = 3 * D_MODEL
Z_CC = N_GATES
Z_CX = Z_CC + CONV_CH
Z_CB = Z_CX + CONV_CH
ZC = Z_CB + CONV_CH
M_Q, M_K, M_V, M_R = 0, GLA_W, 2 * GLA_W, 3 * GLA_W
M_SQ = 4 * GLA_W
M_SK = M_SQ + SWA_QW
M_SV = M_SK + SWA_KW
ZM = M_SV + SWA_KW
NZ = ZC + ZM

LANE = 128
SUBLANE = 8
TOKEN_TILE = 512
MERGE_TILE = 1024
FF_TILE = 512
PROJ_CHUNK = 512
VMEM_LIMIT = 56 * 1024 * 1024


def _rms(x, g):
    return x * lax.rsqrt(jnp.mean(x * x, axis=-1, keepdims=True) + EPS) * g


def _dot(a, b):
    return jnp.dot(a, b, preferred_element_type=F32)


def _dot_nt(a, b):
    return lax.dot_general(a, b, (((1,), (1,)), ((), ())), preferred_element_type=F32)


def _dot_tn(a, b):
    return lax.dot_general(a, b, (((0,), (0,)), ((), ())), preferred_element_type=F32)


def _log_sigmoid(x):
    return jnp.minimum(x, 0.0) - jnp.log(1.0 + jnp.exp(-jnp.abs(x)))


def _split3(x):
    hi = x.astype(BF16)
    r1 = x - hi.astype(F32)
    mid = r1.astype(BF16)
    lo = (r1 - mid.astype(F32)).astype(BF16)
    return hi, mid, lo


def _resident(shape, layer=None):
    nd = len(shape)
    if layer is None:
        return pl.BlockSpec(shape, lambda *_: (0,) * nd, pipeline_mode=pl.Buffered(1))
    return pl.BlockSpec((None,) + tuple(shape), lambda *_: (layer,) + (0,) * nd, pipeline_mode=pl.Buffered(1))


def _params(n_axes):
    return pltpu.CompilerParams(
        dimension_semantics=("arbitrary",) * n_axes, vmem_limit_bytes=VMEM_LIMIT)


def _causal_conv3(y, prev, w):
    n = y.shape[0]
    ext = jnp.concatenate([prev, y], axis=0)
    y1 = ext[SUBLANE - 1:SUBLANE - 1 + n, :]
    y2 = ext[SUBLANE - 2:SUBLANE - 2 + n, :]
    return w[0:1, :] * y2 + w[1:2, :] * y1 + w[2:3, :] * y


def _gla_stages(seq_start, q_ref, k_ref, v_ref, r_ref, ga_ref, wa_hi_ref, wa_lo_ref, ba_ref,
                g_ref, o_ref, st_ref):
    c_len = GLA_CHUNK
    n_chunks = o_ref.shape[0] // c_len

    @pl.when(seq_start)
    def _():
        st_ref[...] = jnp.zeros_like(st_ref)

    row = lax.broadcasted_iota(jnp.int32, (c_len, c_len), 0)
    col = lax.broadcasted_iota(jnp.int32, (c_len, c_len), 1)
    causal = row >= col
    tril = jnp.where(causal, 1.0, 0.0).astype(BF16)
    scale = GLA_DK ** -0.5
    gain = g_ref[...]
    heads = range(GLA_HEADS)
    hsl = [slice(h * GLA_DK, (h + 1) * GLA_DK) for h in heads]
    rsl = [slice(c * c_len, (c + 1) * c_len) for c in range(n_chunks)]

    ga = ga_ref[...]
    ga_hi = ga.astype(BF16)
    ga_lo = (ga - ga_hi.astype(F32)).astype(BF16)
    pre = (_dot(ga_hi, wa_hi_ref[...]) + _dot(ga_lo, wa_hi_ref[...])
           + _dot(ga_hi, wa_lo_ref[...]) + ba_ref[...])
    yield
    la = _log_sigmoid(pre) * (1.0 / GLA_TAU)
    la_c = jnp.concatenate([la[rs, :] for rs in rsl], axis=1)
    la_hi, la_mid, la_lo = _split3(la_c)
    yield
    b_c = _dot(tril, la_hi) + _dot(tril, la_mid) + _dot(tril, la_lo)
    yield

    def scaled(c):
        b = b_c[:, c * GLA_W:(c + 1) * GLA_W]
        b_mid = b[c_len // 2 - 1:c_len // 2, :]
        b_last = b[c_len - 1:c_len, :]
        e_q = jnp.exp(b - b_mid)
        q_in = q_ref[rsl[c], :].astype(F32) * scale * e_q
        k_in = k_ref[rsl[c], :].astype(F32) / e_q
        return dict(
            q_in=q_in.astype(BF16),
            k_in=k_in.astype(BF16),
            q_ex=(q_in * jnp.exp(b_mid)).astype(BF16),
            k_ex=(k_in * jnp.exp(b_last - b_mid)).astype(BF16),
            decay=jnp.exp(b_last),
            v=v_ref[rsl[c], :])

    def scores(s):
        s["attn"] = [_dot_nt(s["q_in"][:, hs], s["k_in"][:, hs]) for hs in hsl]
        s["kv"] = [_dot_tn(s["v"][:, hs], s["k_ex"][:, hs]) for hs in hsl]

    def masked(s):
        s["attn"] = [jnp.where(causal, a, 0.0).astype(BF16) for a in s["attn"]]

    def intra(s):
        s["o"] = [_dot(a, s["v"][:, hs]) for a, hs in zip(s["attn"], hsl)]

    quarters = [list(range(q * n_chunks // 4, (q + 1) * n_chunks // 4)) for q in range(4)]
    parts = {}
    for qt in quarters:
        for c in qt:
            parts[c] = scaled(c)
        yield
        for c in qt:
            scores(parts[c])
    yield
    for qt in quarters:
        for c in qt:
            masked(parts[c])
        yield
        for c in qt:
            intra(parts[c])
    yield

    states = [st_ref[h] for h in heads]
    for c in range(n_chunks):
        s = parts[c]
        inter = [_dot_nt(s["q_ex"][:, hs], states[h].astype(BF16)) for h, hs in enumerate(hsl)]
        states = [states[h] * s["decay"][:, hs] + s["kv"][h] for h, hs in enumerate(hsl)]
        if c % 2 == 0:
            yield
        for h, hs in enumerate(hsl):
            o = s["o"][h] + inter[h]
            o = o * lax.rsqrt(jnp.mean(o * o, axis=-1, keepdims=True) + EPS) * gain[:, hs]
            hr = 0.5 * r_ref[rsl[c], hs].astype(F32)
            o_ref[rsl[c], hs] = (o * (hr + hr * jnp.tanh(hr))).astype(BF16)
    for h in heads:
        st_ref[h] = states[h]


def _swa_stages(seq_start, sink_ref, q_ref, k_ref, v_ref, kp, vp, o_ref):
    w = SWA_WINDOW
    hd = SWA_HEAD_DIM
    grp = SWA_GROUP
    n_blk = q_ref.shape[0] // w
    scale = hd ** -0.5

    row = lax.broadcasted_iota(jnp.int32, (grp * w, 2 * w), 0)
    col = lax.broadcasted_iota(jnp.int32, (grp * w, 2 * w), 1)
    head_in_group = row // w
    dist = w + (row % w) - col
    valid = jnp.logical_and(dist >= 0, dist < w)
    no_prev = jnp.logical_and(seq_start, col < w)
    grp_col = lax.broadcasted_iota(jnp.int32, (grp * w, 1), 0) // w

    bias, bias_first, sinks, keys, vals = [], [], [], [], []
    for h in range(SWA_KV_HEADS):
        slope = jnp.zeros((grp * w, 2 * w), F32)
        sink = jnp.zeros((grp * w, 1), F32)
        for g in range(grp):
            hq = h * grp + g
            slope = jnp.where(head_in_group == g, 2.0 ** (-(8.0 / SWA_Q_HEADS) * (hq + 1)), slope)
            sink = jnp.where(grp_col == g, sink_ref[hq], sink)
        b = jnp.where(valid, -slope * dist.astype(F32), -jnp.inf)
        bias.append(b)
        bias_first.append(jnp.where(no_prev, -jnp.inf, b))
        sinks.append(sink)
        ks = slice(h * hd, (h + 1) * hd)
        keys.append(jnp.concatenate([kp[:, ks], k_ref[:, ks]], axis=0))
        v_h = jnp.concatenate([vp[:, ks], v_ref[:, ks]], axis=0)
        vals.append(jnp.concatenate([v_h, jnp.ones_like(v_h)], axis=1))
    yield

    def scores(h, j):
        qs = [q_ref[j * w:(j + 1) * w, (h * grp + g) * hd:(h * grp + g + 1) * hd] for g in range(grp)]
        q = (jnp.concatenate(qs, axis=0).astype(F32) * scale).astype(BF16)
        return _dot_nt(q, keys[h][j * w:(j + 2) * w, :])

    def finish(h, j, s):
        logits = s + (bias_first[h] if j == 0 else bias[h])
        m = jnp.maximum(jnp.max(logits, axis=-1, keepdims=True), sinks[h])
        p = jnp.exp(logits - m).astype(BF16)
        ov = _dot(p, vals[h][j * w:(j + 2) * w, :])
        o = ov[:, :hd] / (ov[:, hd:] + jnp.exp(sinks[h] - m))
        for g in range(grp):
            hq = h * grp + g
            o_ref[j * w:(j + 1) * w, hq * hd:(hq + 1) * hd] = o[g * w:(g + 1) * w, :].astype(BF16)

    jobs = [(h, j) for h in range(SWA_KV_HEADS) for j in range(n_blk)]
    s_next = scores(*jobs[0])
    for idx, (h, j) in enumerate(jobs):
        s = s_next
        if idx + 1 < len(jobs):
            s_next = scores(*jobs[idx + 1])
        yield
        finish(h, j, s)


def _front_kernel(tiles_per_seq, n_tiles, layer, h_ref, g_ref, w_ref, wga_ref, wa_hi_ref, wa_lo_ref, ba_ref,
                  gn_ref, cw_ref, sink_ref, tg_ref, cv_ref, og_ref, os_ref,
                  zm_ref, ga_scr, kvt_ref, st_ref, ptail_ref):
    i = pl.program_id(0)
    cur = i % 2
    prv = 1 - cur
    n = h_ref.shape[0]
    tile = jnp.minimum(i, n_tiles - 1)

    @pl.when(i == 0)
    def _():
        zm_ref[1] = jnp.zeros(zm_ref.shape[1:], zm_ref.dtype)
        ga_scr[1] = jnp.zeros(ga_scr.shape[1:], ga_scr.dtype)
        kvt_ref[...] = jnp.zeros_like(kvt_ref)
        ptail_ref[...] = jnp.zeros_like(ptail_ref)

    u = _rms(h_ref[...], g_ref[...]).astype(BF16)
    seq_start = jnp.maximum(i - 1, 0) % tiles_per_seq == 0
    zprev = lambda off, width: zm_ref.at[prv, :, off:off + width]
    kv_tail = kvt_ref[...]
    mixers = [
        _gla_stages(seq_start, zprev(M_Q, GLA_W), zprev(M_K, GLA_W), zprev(M_V, GLA_W), zprev(M_R, GLA_W),
                    ga_scr.at[prv], wa_hi_ref, wa_lo_ref, ba_ref, gn_ref, og_ref, st_ref),
        _swa_stages(seq_start, sink_ref.at[layer], zprev(M_SQ, SWA_QW), zprev(M_SK, SWA_KW), zprev(M_SV, SWA_KW),
                    kv_tail[:, :SWA_KW], kv_tail[:, SWA_KW:], os_ref),
    ]

    def advance():
        for gen in list(mixers):
            try:
                next(gen)
            except StopIteration:
                mixers.remove(gen)

    held = {}
    for off in range(0, NZ, PROJ_CHUNK):
        wd = min(PROJ_CHUNK, NZ - off)
        res = _dot(u, w_ref[:, off:off + wd])
        if off < N_GATES:
            tg_ref[:, off:off + wd] = jnp.tanh(0.5 * res).astype(BF16)
        elif off == Z_CC:
            held["cc"] = res
        elif off == Z_CX:
            prod = held.pop("cc") * res
            prev = jnp.where(tile % tiles_per_seq == 0, 0.0, ptail_ref[(tile + 1) % 2])
            ptail_ref[tile % 2] = prod[n - SUBLANE:, :]
            held["conv"] = _causal_conv3(prod, prev, cw_ref[...])
        elif off == Z_CB:
            cv_ref[...] = (res * held.pop("conv")).astype(BF16)
        else:
            zm_ref[cur, :, off - ZC:off - ZC + wd] = res.astype(BF16)
        advance()
    ga_scr[cur] = _dot(u, wga_ref[...])
    while mixers:
        advance()
    kvt_ref[...] = zm_ref[prv, n - SWA_WINDOW:, M_SK:M_SK + 2 * SWA_KW]


def _front(layer, h, g, w_main, w_ga, wa_hi, wa_lo, b_alpha, gain, conv_w, sinks, seq_len):
    t = h.shape[0]
    tm = TOKEN_TILE
    n_tiles = t // tm
    cur_blk = lambda i: (jnp.minimum(i, n_tiles - 1), 0)
    prev_blk = lambda i: (jnp.maximum(i - 1, 0), 0)
    return pl.pallas_call(
        functools.partial(_front_kernel, seq_len // tm, n_tiles, layer),
        grid=(n_tiles + 1,),
        in_specs=[
            pl.BlockSpec((tm, D_MODEL), cur_blk),
            _resident((1, D_MODEL), layer),
            _resident((D_MODEL, NZ), layer),
            _resident((D_MODEL, LANE), layer),
            _resident((LANE, GLA_W), layer), _resident((LANE, GLA_W), layer),
            _resident((1, GLA_W), layer), _resident((1, GLA_W), layer),
            _resident((3, CONV_CH), layer),
            pl.BlockSpec(memory_space=pltpu.SMEM),
        ],
        out_specs=[
            pl.BlockSpec((tm, N_GATES), cur_blk),
            pl.BlockSpec((tm, CONV_CH), cur_blk),
            pl.BlockSpec((tm, GLA_W), prev_blk),
            pl.BlockSpec((tm, SWA_QW), prev_blk),
        ],
        out_shape=[
            jax.ShapeDtypeStruct((t, N_GATES), BF16),
            jax.ShapeDtypeStruct((t, CONV_CH), BF16),
            jax.ShapeDtypeStruct((t, GLA_W), BF16),
            jax.ShapeDtypeStruct((t, SWA_QW), BF16),
        ],
        scratch_shapes=[pltpu.VMEM((2, tm, ZM), BF16),
                        pltpu.VMEM((2, tm, LANE), F32),
                        pltpu.VMEM((SWA_WINDOW, 2 * SWA_KW), BF16),
                        pltpu.VMEM((GLA_HEADS, GLA_DV, GLA_DK), F32),
                        pltpu.VMEM((2, SUBLANE, CONV_CH), F32)],
        compiler_params=_params(1),
        name="front",
    )(h, g, w_main, w_ga, wa_hi, wa_lo, b_alpha, gain, conv_w, sinks)


def _merge_kernel(h_ref, og_ref, os_ref, cv_ref, t1_ref, t2_ref, t3_ref,
                  wg_ref, wc_ref, ws_ref, wo_ref, o_ref):
    y_gla = _dot(og_ref[...], wg_ref[...])
    y_conv = _dot(cv_ref[...], wc_ref[...])
    y_swa = _dot(os_ref[...], ws_ref[...])

    def gate(t_ref, y):
        return y + t_ref[...].astype(F32) * y

    merged = 0.5 * (gate(t1_ref, y_gla) + gate(t2_ref, y_conv) + gate(t3_ref, y_swa))
    o_ref[...] = h_ref[...] + _dot(merged.astype(BF16), wo_ref[...])


def _merge(layer, h, o_gla, o_swa, cv, tg, w_gla_o, w_conv_o, w_swa_o, w_o):
    t = h.shape[0]
    tm = MERGE_TILE
    row = lambda width, j=0: pl.BlockSpec((tm, width), lambda i: (i, j))
    return pl.pallas_call(
        _merge_kernel,
        grid=(t // tm,),
        in_specs=[
            row(D_MODEL), row(GLA_W), row(SWA_QW), row(CONV_CH),
            row(D_MODEL, 0), row(D_MODEL, 1), row(D_MODEL, 2),
            _resident((GLA_W, D_MODEL), layer),
            _resident((CONV_CH, D_MODEL), layer),
            _resident((SWA_QW, D_MODEL), layer),
            _resident((D_MODEL, D_MODEL), layer),
        ],
        out_specs=row(D_MODEL),
        out_shape=jax.ShapeDtypeStruct((t, D_MODEL), F32),
        compiler_params=_params(1),
        name="merge",
    )(h, o_gla, o_swa, cv, tg, tg, tg, w_gla_o, w_conv_o, w_swa_o, w_o)


def _ffn_kernel(tiles_per_seq, final_norm, h_ref, g_ref, wup_ref, cw_ref, wd_ref, gf_ref, o_ref,
                act_ref, tail_ref):
    @pl.when(pl.program_id(0) % tiles_per_seq == 0)
    def _():
        tail_ref[...] = jnp.zeros_like(tail_ref)

    x = h_ref[...]
    u = _rms(x, g_ref[...]).astype(BF16)
    n = x.shape[0]
    nf = -(-D_FF // FF_TILE)
    gcol = lambda c: slice(c * FF_TILE, min((c + 1) * FF_TILE, D_FF))
    vcol = lambda c: slice(D_FF + c * FF_TILE, D_FF + min((c + 1) * FF_TILE, D_FF))

    def up(c):
        return _dot(u, wup_ref[:, gcol(c)]), _dot(u, wup_ref[:, vcol(c)])

    def gated(c, hid):
        hg, hv = hid
        pg = tail_ref[0, :, gcol(c)]
        pv = tail_ref[1, :, gcol(c)]
        tail_ref[0, :, gcol(c)] = hg[n - SUBLANE:, :]
        tail_ref[1, :, gcol(c)] = hv[n - SUBLANE:, :]
        s = _causal_conv3(hg, pg, 0.5 * cw_ref[:, gcol(c)])
        v = _causal_conv3(hv, pv, cw_ref[:, vcol(c)])
        act_ref[:, gcol(c)] = ((s + s * jnp.tanh(s)) * v).astype(BF16)

    hid = up(0)
    for c in range(nf):
        nxt = up(c + 1) if c + 1 < nf else None
        gated(c, hid)
        hid = nxt
    out = x + _dot(act_ref[...], wd_ref[...])
    if final_norm:
        out = _rms(out, gf_ref[...])
    o_ref[...] = out


def _ffn(layer, h, g, wup, cw, wd, g_final, seq_len, final_norm):
    t = h.shape[0]
    tm = TOKEN_TILE
    return pl.pallas_call(
        functools.partial(_ffn_kernel, seq_len // tm, final_norm),
        grid=(t // tm,),
        in_specs=[
            pl.BlockSpec((tm, D_MODEL), lambda i: (i, 0)),
            _resident((1, D_MODEL), layer),
            _resident((D_MODEL, 2 * D_FF), layer),
            _resident((3, 2 * D_FF), layer),
            _resident((D_FF, D_MODEL), layer),
            _resident((1, D_MODEL)),
        ],
        out_specs=pl.BlockSpec((tm, D_MODEL), lambda i: (i, 0)),
        out_shape=jax.ShapeDtypeStruct((t, D_MODEL), F32),
        scratch_shapes=[pltpu.VMEM((tm, D_FF), BF16),
                        pltpu.VMEM((2, SUBLANE, D_FF), F32)],
        compiler_params=_params(1),
        name="ffn",
    )(h, g, wup, cw, wd, g_final)


def _prep_weights(w_in, w_alpha):
    w_in = w_in.astype(BF16)
    w_main = jnp.concatenate(
        [w_in[..., _SRC_GATES:_SRC_END],
         w_in[..., _SRC_CONV + 2 * CONV_CH:_SRC_SWA],
         w_in[..., _SRC_CONV:_SRC_CONV + CONV_CH],
         w_in[..., _SRC_CONV + CONV_CH:_SRC_CONV + 2 * CONV_CH],
         w_in[..., _SRC_GLA:_SRC_GA], w_in[..., _SRC_SWA:_SRC_GATES]], axis=-1)
    w_ga = jnp.pad(w_in[..., _SRC_GA:_SRC_CONV], ((0, 0), (0, 0), (0, LANE - GLA_RANK)))
    wa = jnp.pad(w_alpha, ((0, 0), (0, LANE - GLA_RANK), (0, 0)))
    wa_hi = wa.astype(BF16)
    wa_lo = (wa - wa_hi.astype(F32)).astype(BF16)
    return w_main, w_ga, wa_hi, wa_lo


def kernel(x, g_mix, w_in, gla_w_alpha, gla_b_alpha, gla_norm_g, conv_w, swa_sinks, w_gla_o,
           w_conv_o, w_swa_o, w_o, g_ffn, w_up, ffn_conv_w, w_down, g_final):
    bsz, seq_len, d = x.shape
    assert d == D_MODEL and seq_len % TOKEN_TILE == 0 and (bsz * seq_len) % MERGE_TILE == 0
    depth = w_in.shape[0]
    w_main, w_ga, wa_hi, wa_lo = _prep_weights(w_in, gla_w_alpha)
    row = lambda p: p[:, None, :]
    g_mix, b_alpha, gain, g_ffn = row(g_mix), row(gla_b_alpha), row(gla_norm_g), row(g_ffn)
    w_gla_o, w_conv_o, w_swa_o, w_o = (w.astype(BF16) for w in (w_gla_o, w_conv_o, w_swa_o, w_o))
    w_up, w_down = w_up.astype(BF16), w_down.astype(BF16)
    h = x.reshape(bsz * seq_len, d)
    for l in range(depth):
        tg, cv, o_gla, o_swa = _front(l, h, g_mix, w_main, w_ga, wa_hi, wa_lo, b_alpha, gain, conv_w,
                                      swa_sinks, seq_len)
        h = _merge(l, h, o_gla, o_swa, cv, tg, w_gla_o, w_conv_o, w_swa_o, w_o)
        h = _ffn(l, h, g_ffn, w_up, ffn_conv_w, w_down, g_final[None, :], seq_len, final_norm=(l == depth - 1))
    return h.reshape(bsz, seq_len, d)
```

```python
import functools

import jax
import jax.numpy as jnp
from jax import lax
from jax.experimental import pallas as pl
from jax.experimental.pallas import tpu as pltpu

F32 = jnp.float32
BF16 = jnp.bfloat16

D_MODEL = 1024
GLA_HEADS = 4
GLA_DK = 128
GLA_DV = 128
GLA_RANK = 16
GLA_TAU = 16.0
GLA_CHUNK = 64
CONV_CH = D_MODEL // 2
SWA_Q_HEADS = 8
SWA_KV_HEADS = 2
SWA_GROUP = SWA_Q_HEADS // SWA_KV_HEADS
SWA_HEAD_DIM = 64
SWA_WINDOW = 128
D_FF = 2816
EPS = 1e-6

GLA_W = GLA_HEADS * GLA_DK
SWA_QW = SWA_Q_HEADS * SWA_HEAD_DIM
SWA_KW = SWA_KV_HEADS * SWA_HEAD_DIM

_SRC_GLA = 0
_SRC_GA = 4 * GLA_W
_SRC_CONV = _SRC_GA + GLA_RANK
_SRC_SWA = _SRC_CONV + 3 * CONV_CH
_SRC_GATES = _SRC_SWA + SWA_QW + 2 * SWA_KW
_SRC_END = _SRC_GATES + 3 * D_MODEL

Z_GATES = 0
N_GATES = 3 * D_MODEL
Z_CC = N_GATES
Z_CX = Z_CC + CONV_CH
Z_CB = Z_CX + CONV_CH
ZC = Z_CB + CONV_CH
M_Q, M_K, M_V, M_R = 0, GLA_W, 2 * GLA_W, 3 * GLA_W
M_SQ = 4 * GLA_W
M_SK = M_SQ + SWA_QW
M_SV = M_SK + SWA_KW
ZM = M_SV + SWA_KW
NZ = ZC + ZM

LANE = 128
SUBLANE = 8
TOKEN_TILE = 512
MERGE_TILE = 1024
FF_TILE = 512
PROJ_CHUNK = 512
VMEM_LIMIT = 56 * 1024 * 1024


def _rms(x, g):
    return x * lax.rsqrt(jnp.mean(x * x, axis=-1, keepdims=True) + EPS) * g


def _dot(a, b):
    return jnp.dot(a, b, preferred_element_type=F32)


def _dot_nt(a, b):
    return lax.dot_general(a, b, (((1,), (1,)), ((), ())), preferred_element_type=F32)


def _dot_tn(a, b):
    return lax.dot_general(a, b, (((0,), (0,)), ((), ())), preferred_element_type=F32)


def _log_sigmoid(x):
    return jnp.minimum(x, 0.0) - jnp.log(1.0 + jnp.exp(-jnp.abs(x)))


def _split3(x):
    hi = x.astype(BF16)
    r1 = x - hi.astype(F32)
    mid = r1.astype(BF16)
    lo = (r1 - mid.astype(F32)).astype(BF16)
    return hi, mid, lo


def _resident(shape, layer=None):
    nd = len(shape)
    if layer is None:
        return pl.BlockSpec(shape, lambda *_: (0,) * nd, pipeline_mode=pl.Buffered(1))
    return pl.BlockSpec((None,) + tuple(shape), lambda *_: (layer,) + (0,) * nd, pipeline_mode=pl.Buffered(1))


def _params(n_axes):
    return pltpu.CompilerParams(
        dimension_semantics=("arbitrary",) * n_axes, vmem_limit_bytes=VMEM_LIMIT)


def _causal_conv3(y, prev, w):
    n = y.shape[0]
    ext = jnp.concatenate([prev, y], axis=0)
    y1 = ext[SUBLANE - 1:SUBLANE - 1 + n, :]
    y2 = ext[SUBLANE - 2:SUBLANE - 2 + n, :]
    return w[0:1, :] * y2 + w[1:2, :] * y1 + w[2:3, :] * y


def _gla_stages(seq_start, q_ref, k_ref, v_ref, r_ref, ga_ref, wa_hi_ref, wa_lo_ref, ba_ref,
                g_ref, o_ref, st_ref):
    c_len = GLA_CHUNK
    n_chunks = o_ref.shape[0] // c_len

    @pl.when(seq_start)
    def _():
        st_ref[...] = jnp.zeros_like(st_ref)

    row = lax.broadcasted_iota(jnp.int32, (c_len, c_len), 0)
    col = lax.broadcasted_iota(jnp.int32, (c_len, c_len), 1)
    causal = row >= col
    tril = jnp.where(causal, 1.0, 0.0).astype(BF16)
    scale = GLA_DK ** -0.5
    gain = g_ref[...]
    heads = range(GLA_HEADS)
    hsl = [slice(h * GLA_DK, (h + 1) * GLA_DK) for h in heads]
    rsl = [slice(c * c_len, (c + 1) * c_len) for c in range(n_chunks)]

    ga = ga_ref[...]
    ga_hi = ga.astype(BF16)
    ga_lo = (ga - ga_hi.astype(F32)).astype(BF16)
    pre = (_dot(ga_hi, wa_hi_ref[...]) + _dot(ga_lo, wa_hi_ref[...])
           + _dot(ga_hi, wa_lo_ref[...]) + ba_ref[...])
    yield
    la = _log_sigmoid(pre) * (1.0 / GLA_TAU)
    la_c = jnp.concatenate([la[rs, :] for rs in rsl], axis=1)
    la_hi, la_mid, la_lo = _split3(la_c)
    yield
    b_c = _dot(tril, la_hi) + _dot(tril, la_mid) + _dot(tril, la_lo)
    yield

    def scaled(c):
        b = b_c[:, c * GLA_W:(c + 1) * GLA_W]
        b_mid = b[c_len // 2 - 1:c_len // 2, :]
        b_last = b[c_len - 1:c_len, :]
        e_q = jnp.exp(b - b_mid)
        q_in = q_ref[rsl[c], :].astype(F32) * scale * e_q
        k_in = k_ref[rsl[c], :].astype(F32) / e_q
        return dict(
            q_in=q_in.astype(BF16),
            k_in=k_in.astype(BF16),
            q_ex=(q_in * jnp.exp(b_mid)).astype(BF16),
            k_ex=(k_in * jnp.exp(b_last - b_mid)).astype(BF16),
            decay=jnp.exp(b_last),
            v=v_ref[rsl[c], :])

    def scores(s):
        s["attn"] = [_dot_nt(s["q_in"][:, hs], s["k_in"][:, hs]) for hs in hsl]
        s["kv"] = [_dot_tn(s["v"][:, hs], s["k_ex"][:, hs]) for hs in hsl]

    def masked(s):
        s["attn"] = [jnp.where(causal, a, 0.0).astype(BF16) for a in s["attn"]]

    def intra(s):
        s["o"] = [_dot(a, s["v"][:, hs]) for a, hs in zip(s["attn"], hsl)]

    quarters = [list(range(q * n_chunks // 4, (q + 1) * n_chunks // 4)) for q in range(4)]
    parts = {}
    for qt in quarters:
        for c in qt:
            parts[c] = scaled(c)
        yield
        for c in qt:
            scores(parts[c])
    yield
    for qt in quarters:
        for c in qt:
            masked(parts[c])
        yield
        for c in qt:
            intra(parts[c])
    yield

    states = [st_ref[h] for h in heads]
    for c in range(n_chunks):
        s = parts[c]
        inter = [_dot_nt(s["q_ex"][:, hs], states[h].astype(BF16)) for h, hs in enumerate(hsl)]
        states = [states[h] * s["decay"][:, hs] + s["kv"][h] for h, hs in enumerate(hsl)]
        if c % 2 == 0:
            yield
        for h, hs in enumerate(hsl):
            o = s["o"][h] + inter[h]
            o = o * lax.rsqrt(jnp.mean(o * o, axis=-1, keepdims=True) + EPS) * gain[:, hs]
            hr = 0.5 * r_ref[rsl[c], hs].astype(F32)
            o_ref[rsl[c], hs] = (o * (hr + hr * jnp.tanh(hr))).astype(BF16)
    for h in heads:
        st_ref[h] = states[h]


def _swa_stages(seq_start, sink_ref, q_ref, k_ref, v_ref, kp, vp, o_ref):
    w = SWA_WINDOW
    hd = SWA_HEAD_DIM
    grp = SWA_GROUP
    n_blk = q_ref.shape[0] // w
    scale = hd ** -0.5

    row = lax.broadcasted_iota(jnp.int32, (grp * w, 2 * w), 0)
    col = lax.broadcasted_iota(jnp.int32, (grp * w, 2 * w), 1)
    head_in_group = row // w
    dist = w + (row % w) - col
    valid = jnp.logical_and(dist >= 0, dist < w)
    no_prev = jnp.logical_and(seq_start, col < w)
    grp_col = lax.broadcasted_iota(jnp.int32, (grp * w, 1), 0) // w

    bias, bias_first, sinks, keys, vals = [], [], [], [], []
    for h in range(SWA_KV_HEADS):
        slope = jnp.zeros((grp * w, 2 * w), F32)
        sink = jnp.zeros((grp * w, 1), F32)
        for g in range(grp):
            hq = h * grp + g
            slope = jnp.where(head_in_group == g, 2.0 ** (-(8.0 / SWA_Q_HEADS) * (hq + 1)), slope)
            sink = jnp.where(grp_col == g, sink_ref[hq], sink)
        b = jnp.where(valid, -slope * dist.astype(F32), -jnp.inf)
        bias.append(b)
        bias_first.append(jnp.where(no_prev, -jnp.inf, b))
        sinks.append(sink)
        ks = slice(h * hd, (h + 1) * hd)
        keys.append(jnp.concatenate([kp[:, ks], k_ref[:, ks]], axis=0))
        v_h = jnp.concatenate([vp[:, ks], v_ref[:, ks]], axis=0)
        vals.append(jnp.concatenate([v_h, jnp.ones_like(v_h)], axis=1))
    yield

    def scores(h, j):
        qs = [q_ref[j * w:(j + 1) * w, (h * grp + g) * hd:(h * grp + g + 1) * hd] for g in range(grp)]
        q = (jnp.concatenate(qs, axis=0).astype(F32) * scale).astype(BF16)
        return _dot_nt(q, keys[h][j * w:(j + 2) * w, :])

    def finish(h, j, s):
        logits = s + (bias_first[h] if j == 0 else bias[h])
        m = jnp.maximum(jnp.max(logits, axis=-1, keepdims=True), sinks[h])
        p = jnp.exp(logits - m).astype(BF16)
        ov = _dot(p, vals[h][j * w:(j + 2) * w, :])
        o = ov[:, :hd] / (ov[:, hd:] + jnp.exp(sinks[h] - m))
        for g in range(grp):
            hq = h * grp + g
            o_ref[j * w:(j + 1) * w, hq * hd:(hq + 1) * hd] = o[g * w:(g + 1) * w, :].astype(BF16)

    jobs = [(h, j) for h in range(SWA_KV_HEADS) for j in range(n_blk)]
    s_next = scores(*jobs[0])
    for idx, (h, j) in enumerate(jobs):
        s = s_next
        if idx + 1 < len(jobs):
            s_next = scores(*jobs[idx + 1])
        yield
        finish(h, j, s)


def _front_kernel(tiles_per_seq, n_tiles, layer, h_ref, g_ref, w_ref, wga_ref, wa_hi_ref, wa_lo_ref, ba_ref,
                  gn_ref, cw_ref, sink_ref, tg_ref, cv_ref, og_ref, os_ref,
                  zm_ref, ga_scr, kvt_ref, st_ref, ptail_ref):
    i = pl.program_id(0)
    cur = i % 2
    prv = 1 - cur
    n = h_ref.shape[0]
    tile = jnp.minimum(i, n_tiles - 1)

    @pl.when(i == 0)
    def _():
        zm_ref[1] = jnp.zeros(zm_ref.shape[1:], zm_ref.dtype)
        ga_scr[1] = jnp.zeros(ga_scr.shape[1:], ga_scr.dtype)
        kvt_ref[...] = jnp.zeros_like(kvt_ref)
        ptail_ref[...] = jnp.zeros_like(ptail_ref)

    u = _rms(h_ref[...], g_ref[...]).astype(BF16)
    seq_start = jnp.maximum(i - 1, 0) % tiles_per_seq == 0
    zprev = lambda off, width: zm_ref.at[prv, :, off:off + width]
    kv_tail = kvt_ref[...]
    mixers = [
        _gla_stages(seq_start, zprev(M_Q, GLA_W), zprev(M_K, GLA_W), zprev(M_V, GLA_W), zprev(M_R, GLA_W),
                    ga_scr.at[prv], wa_hi_ref, wa_lo_ref, ba_ref, gn_ref, og_ref, st_ref),
        _swa_stages(seq_start, sink_ref.at[layer], zprev(M_SQ, SWA_QW), zprev(M_SK, SWA_KW), zprev(M_SV, SWA_KW),
                    kv_tail[:, :SWA_KW], kv_tail[:, SWA_KW:], os_ref),
    ]

    def advance():
        for gen in list(mixers):
            try:
                next(gen)
            except StopIteration:
                mixers.remove(gen)

    held = {}
    for off in range(0, NZ, PROJ_CHUNK):
        wd = min(PROJ_CHUNK, NZ - off)
        res = _dot(u, w_ref[:, off:off + wd])
        if off < N_GATES:
            tg_ref[:, off:off + wd] = jnp.tanh(0.5 * res).astype(BF16)
        elif off == Z_CC:
            held["cc"] = res
        elif off == Z_CX:
            prod = held.pop("cc") * res
            prev = jnp.where(tile % tiles_per_seq == 0, 0.0, ptail_ref[(tile + 1) % 2])
            ptail_ref[tile % 2] = prod[n - SUBLANE:, :]
            held["conv"] = _causal_conv3(prod, prev, cw_ref[...])
        elif off == Z_CB:
            cv_ref[...] = (res * held.pop("conv")).astype(BF16)
        else:
            zm_ref[cur, :, off - ZC:off - ZC + wd] = res.astype(BF16)
        advance()
    ga_scr[cur] = _dot(u, wga_ref[...])
    while mixers:
        advance()
    kvt_ref[...] = zm_ref[prv, n - SWA_WINDOW:, M_SK:M_SK + 2 * SWA_KW]


def _front(layer, h, g, w_main, w_ga, wa_hi, wa_lo, b_alpha, gain, conv_w, sinks, seq_len):
    t = h.shape[0]
    tm = TOKEN_TILE
    n_tiles = t // tm
    cur_blk = lambda i: (jnp.minimum(i, n_tiles - 1), 0)
    prev_blk = lambda i: (jnp.maximum(i - 1, 0), 0)
    return pl.pallas_call(
        functools.partial(_front_kernel, seq_len // tm, n_tiles, layer),
        grid=(n_tiles + 1,),
        in_specs=[
            pl.BlockSpec((tm, D_MODEL), cur_blk),
            _resident((1, D_MODEL), layer),
            _resident((D_MODEL, NZ), layer),
            _resident((D_MODEL, LANE), layer),
            _resident((LANE, GLA_W), layer), _resident((LANE, GLA_W), layer),
            _resident((1, GLA_W), layer), _resident((1, GLA_W), layer),
            _resident((3, CONV_CH), layer),
            pl.BlockSpec(memory_space=pltpu.SMEM),
        ],
        out_specs=[
            pl.BlockSpec((tm, N_GATES), cur_blk),
            pl.BlockSpec((tm, CONV_CH), cur_blk),
            pl.BlockSpec((tm, GLA_W), prev_blk),
            pl.BlockSpec((tm, SWA_QW), prev_blk),
        ],
        out_shape=[
            jax.ShapeDtypeStruct((t, N_GATES), BF16),
            jax.ShapeDtypeStruct((t, CONV_CH), BF16),
            jax.ShapeDtypeStruct((t, GLA_W), BF16),
            jax.ShapeDtypeStruct((t, SWA_QW), BF16),
        ],
        scratch_shapes=[pltpu.VMEM((2, tm, ZM), BF16),
                        pltpu.VMEM((2, tm, LANE), F32),
                        pltpu.VMEM((SWA_WINDOW, 2 * SWA_KW), BF16),
                        pltpu.VMEM((GLA_HEADS, GLA_DV, GLA_DK), F32),
                        pltpu.VMEM((2, SUBLANE, CONV_CH), F32)],
        compiler_params=_params(1),
        name="front",
    )(h, g, w_main, w_ga, wa_hi, wa_lo, b_alpha, gain, conv_w, sinks)


def _merge_kernel(h_ref, og_ref, os_ref, cv_ref, t1_ref, t2_ref, t3_ref,
                  wg_ref, wc_ref, ws_ref, wo_ref, o_ref):
    y_gla = _dot(og_ref[...], wg_ref[...])
    y_conv = _dot(cv_ref[...], wc_ref[...])
    y_swa = _dot(os_ref[...], ws_ref[...])

    def gate(t_ref, y):
        return y + t_ref[...].astype(F32) * y

    merged = 0.5 * (gate(t1_ref, y_gla) + gate(t2_ref, y_conv) + gate(t3_ref, y_swa))
    o_ref[...] = h_ref[...] + _dot(merged.astype(BF16), wo_ref[...])


def _merge(layer, h, o_gla, o_swa, cv, tg, w_gla_o, w_conv_o, w_swa_o, w_o):
    t = h.shape[0]
    tm = MERGE_TILE
    row = lambda width, j=0: pl.BlockSpec((tm, width), lambda i: (i, j))
    return pl.pallas_call(
        _merge_kernel,
        grid=(t // tm,),
        in_specs=[
            row(D_MODEL), row(GLA_W), row(SWA_QW), row(CONV_CH),
            row(D_MODEL, 0), row(D_MODEL, 1), row(D_MODEL, 2),
            _resident((GLA_W, D_MODEL), layer),
            _resident((CONV_CH, D_MODEL), layer),
            _resident((SWA_QW, D_MODEL), layer),
            _resident((D_MODEL, D_MODEL), layer),
        ],
        out_specs=row(D_MODEL),
        out_shape=jax.ShapeDtypeStruct((t, D_MODEL), F32),
        compiler_params=_params(1),
        name="merge",
    )(h, o_gla, o_swa, cv, tg, tg, tg, w_gla_o, w_conv_o, w_swa_o, w_o)


def _ffn_kernel(tiles_per_seq, final_norm, h_ref, g_ref, wup_ref, cw_ref, wd_ref, gf_ref, o_ref,
                act_ref, tail_ref):
    @pl.when(pl.program_id(0) % tiles_per_seq == 0)
    def _():
        tail_ref[...] = jnp.zeros_like(tail_ref)

    x = h_ref[...]
    u = _rms(x, g_ref[...]).astype(BF16)
    n = x.shape[0]
    nf = -(-D_FF // FF_TILE)
    gcol = lambda c: slice(c * FF_TILE, min((c + 1) * FF_TILE, D_FF))
    vcol = lambda c: slice(D_FF + c * FF_TILE, D_FF + min((c + 1) * FF_TILE, D_FF))

    def up(c):
        return _dot(u, wup_ref[:, gcol(c)]), _dot(u, wup_ref[:, vcol(c)])

    def gated(c, hid):
        hg, hv = hid
        pg = tail_ref[0, :, gcol(c)]
        pv = tail_ref[1, :, gcol(c)]
        tail_ref[0, :, gcol(c)] = hg[n - SUBLANE:, :]
        tail_ref[1, :, gcol(c)] = hv[n - SUBLANE:, :]
        s = _causal_conv3(hg, pg, 0.5 * cw_ref[:, gcol(c)])
        v = _causal_conv3(hv, pv, cw_ref[:, vcol(c)])
        act_ref[:, gcol(c)] = ((s + s * jnp.tanh(s)) * v).astype(BF16)

    hid = up(0)
    for c in range(nf):
        nxt = up(c + 1) if c + 1 < nf else None
        gated(c, hid)
        hid = nxt
    out = x + _dot(act_ref[...], wd_ref[...])
    if final_norm:
        out = _rms(out, gf_ref[...])
    o_ref[...] = out


def _ffn(layer, h, g, wup, cw, wd, g_final, seq_len, final_norm):
    t = h.shape[0]
    tm = TOKEN_TILE
    return pl.pallas_call(
        functools.partial(_ffn_kernel, seq_len // tm, final_norm),
        grid=(t // tm,),
        in_specs=[
            pl.BlockSpec((tm, D_MODEL), lambda i: (i, 0)),
            _resident((1, D_MODEL), layer),
            _resident((D_MODEL, 2 * D_FF), layer),
            _resident((3, 2 * D_FF), layer),
            _resident((D_FF, D_MODEL), layer),
            _resident((1, D_MODEL)),
        ],
        out_specs=pl.BlockSpec((tm, D_MODEL), lambda i: (i, 0)),
        out_shape=jax.ShapeDtypeStruct((t, D_MODEL), F32),
        scratch_shapes=[pltpu.VMEM((tm, D_FF), BF16),
                        pltpu.VMEM((2, SUBLANE, D_FF), F32)],
        compiler_params=_params(1),
        name="ffn",
    )(h, g, wup, cw, wd, g_final)


_W_PIECES = (
    (_SRC_GATES, 3 * D_MODEL),
    (_SRC_CONV + 2 * CONV_CH, CONV_CH),
    (_SRC_CONV, CONV_CH),
    (_SRC_CONV + CONV_CH, CONV_CH),
    (_SRC_GLA, 4 * GLA_W),
    (_SRC_SWA, SWA_QW + 2 * SWA_KW),
)
W_ROWS = 256


def _relayout_kernel(w_ref, wm_ref, wga_ref):
    off = 0
    for src, width in _W_PIECES:
        wm_ref[:, off:off + width] = w_ref[:, src:src + width].astype(BF16)
        off += width
    lane = lax.broadcasted_iota(jnp.int32, wga_ref.shape, 1)
    wga_ref[...] = jnp.where(lane < GLA_RANK, w_ref[:, _SRC_GA:_SRC_GA + LANE], 0.0).astype(BF16)


def _relayout(w_in):
    depth, k, n = w_in.shape
    return pl.pallas_call(
        _relayout_kernel,
        grid=(depth, k // W_ROWS),
        in_specs=[pl.BlockSpec((None, W_ROWS, n), lambda l, i: (l, i, 0))],
        out_specs=[pl.BlockSpec((None, W_ROWS, NZ), lambda l, i: (l, i, 0)),
                   pl.BlockSpec((None, W_ROWS, LANE), lambda l, i: (l, i, 0))],
        out_shape=[jax.ShapeDtypeStruct((depth, k, NZ), BF16), jax.ShapeDtypeStruct((depth, k, LANE), BF16)],
        compiler_params=_params(2),
        name="relayout",
    )(w_in)


def _prep_weights(w_in, w_alpha):
    w_main, w_ga = _relayout(w_in)
    wa = jnp.pad(w_alpha, ((0, 0), (0, LANE - GLA_RANK), (0, 0)))
    wa_hi = wa.astype(BF16)
    wa_lo = (wa - wa_hi.astype(F32)).astype(BF16)
    return w_main, w_ga, wa_hi, wa_lo


def kernel(x, g_mix, w_in, gla_w_alpha, gla_b_alpha, gla_norm_g, conv_w, swa_sinks, w_gla_o,
           w_conv_o, w_swa_o, w_o, g_ffn, w_up, ffn_conv_w, w_down, g_final):
    bsz, seq_len, d = x.shape
    assert d == D_MODEL and seq_len % TOKEN_TILE == 0 and (bsz * seq_len) % MERGE_TILE == 0
    depth = w_in.shape[0]
    w_main, w_ga, wa_hi, wa_lo = _prep_weights(w_in, gla_w_alpha)
    row = lambda p: p[:, None, :]
    g_mix, b_alpha, gain, g_ffn = row(g_mix), row(gla_b_alpha), row(gla_norm_g), row(g_ffn)
    w_gla_o, w_conv_o, w_swa_o, w_o = (w.astype(BF16) for w in (w_gla_o, w_conv_o, w_swa_o, w_o))
    w_up, w_down = w_up.astype(BF16), w_down.astype(BF16)
    h = x.reshape(bsz * seq_len, d)
    for l in range(depth):
        tg, cv, o_gla, o_swa = _front(l, h, g_mix, w_main, w_ga, wa_hi, wa_lo, b_alpha, gain, conv_w,
                                      swa_sinks, seq_len)
        h = _merge(l, h, o_gla, o_swa, cv, tg, w_gla_o, w_conv_o, w_swa_o, w_o)
        h = _ffn(l, h, g_ffn, w_up, ffn_conv_w, w_down, g_final[None, :], seq_len, final_norm=(l == depth - 1))
    return h.reshape(bsz, seq_len, d)
```

```python
import functools

import jax
import jax.numpy as jnp
from jax import lax
from jax.experimental import pallas as pl
from jax.experimental.pallas import tpu as pltpu

F32 = jnp.float32
BF16 = jnp.bfloat16

D_MODEL = 1024
GLA_HEADS = 4
GLA_DK = 128
GLA_DV = 128
GLA_RANK = 16
GLA_TAU = 16.0
GLA_CHUNK = 64
CONV_CH = D_MODEL // 2
SWA_Q_HEADS = 8
SWA_KV_HEADS = 2
SWA_GROUP = SWA_Q_HEADS // SWA_KV_HEADS
SWA_HEAD_DIM = 64
SWA_WINDOW = 128
D_FF = 2816
EPS = 1e-6

GLA_W = GLA_HEADS * GLA_DK
SWA_QW = SWA_Q_HEADS * SWA_HEAD_DIM
SWA_KW = SWA_KV_HEADS * SWA_HEAD_DIM

_SRC_GLA = 0
_SRC_GA = 4 * GLA_W
_SRC_CONV = _SRC_GA + GLA_RANK
_SRC_SWA = _SRC_CONV + 3 * CONV_CH
_SRC_GATES = _SRC_SWA + SWA_QW + 2 * SWA_KW
_SRC_END = _SRC_GATES + 3 * D_MODEL

Z_GATES = 0
N_GATES = 3 * D_MODEL
Z_CC = N_GATES
Z_CX = Z_CC + CONV_CH
Z_CB = Z_CX + CONV_CH
ZC = Z_CB + CONV_CH
M_Q, M_K, M_V, M_R = 0, GLA_W, 2 * GLA_W, 3 * GLA_W
M_SQ = 4 * GLA_W
M_SK = M_SQ + SWA_QW
M_SV = M_SK + SWA_KW
ZM = M_SV + SWA_KW
NZ = ZC + ZM

LANE = 128
SUBLANE = 8
TOKEN_TILE = 512
MERGE_TILE = 1024
FF_TILE = 512
PROJ_CHUNK = 512
VMEM_LIMIT = 56 * 1024 * 1024


def _rms(x, g):
    return x * lax.rsqrt(jnp.mean(x * x, axis=-1, keepdims=True) + EPS) * g


def _dot(a, b):
    return jnp.dot(a, b, preferred_element_type=F32)


def _dot_nt(a, b):
    return lax.dot_general(a, b, (((1,), (1,)), ((), ())), preferred_element_type=F32)


def _dot_tn(a, b):
    return lax.dot_general(a, b, (((0,), (0,)), ((), ())), preferred_element_type=F32)


def _log_sigmoid(x):
    return jnp.minimum(x, 0.0) - jnp.log(1.0 + jnp.exp(-jnp.abs(x)))


def _split3(x):
    hi = x.astype(BF16)
    r1 = x - hi.astype(F32)
    mid = r1.astype(BF16)
    lo = (r1 - mid.astype(F32)).astype(BF16)
    return hi, mid, lo


def _resident(shape, layer=None):
    nd = len(shape)
    if layer is None:
        return pl.BlockSpec(shape, lambda *_: (0,) * nd, pipeline_mode=pl.Buffered(1))
    return pl.BlockSpec((None,) + tuple(shape), lambda *_: (layer,) + (0,) * nd, pipeline_mode=pl.Buffered(1))


def _params(n_axes):
    return pltpu.CompilerParams(
        dimension_semantics=("arbitrary",) * n_axes, vmem_limit_bytes=VMEM_LIMIT)


def _causal_conv3(y, prev, w):
    n = y.shape[0]
    ext = jnp.concatenate([prev, y], axis=0)
    y1 = ext[SUBLANE - 1:SUBLANE - 1 + n, :]
    y2 = ext[SUBLANE - 2:SUBLANE - 2 + n, :]
    return w[0:1, :] * y2 + w[1:2, :] * y1 + w[2:3, :] * y


def _gla_stages(seq_start, q_ref, k_ref, v_ref, r_ref, ga_ref, wa_hi_ref, wa_lo_ref, ba_ref,
                g_ref, o_ref, st_ref):
    c_len = GLA_CHUNK
    n_chunks = o_ref.shape[0] // c_len

    @pl.when(seq_start)
    def _():
        st_ref[...] = jnp.zeros_like(st_ref)

    row = lax.broadcasted_iota(jnp.int32, (c_len, c_len), 0)
    col = lax.broadcasted_iota(jnp.int32, (c_len, c_len), 1)
    causal = row >= col
    tril = jnp.where(causal, 1.0, 0.0).astype(BF16)
    scale = GLA_DK ** -0.5
    gain = g_ref[...]
    heads = range(GLA_HEADS)
    hsl = [slice(h * GLA_DK, (h + 1) * GLA_DK) for h in heads]
    rsl = [slice(c * c_len, (c + 1) * c_len) for c in range(n_chunks)]

    ga = ga_ref[...]
    ga_hi = ga.astype(BF16)
    ga_lo = (ga - ga_hi.astype(F32)).astype(BF16)
    pre = (_dot(ga_hi, wa_hi_ref[...]) + _dot(ga_lo, wa_hi_ref[...])
           + _dot(ga_hi, wa_lo_ref[...]) + ba_ref[...])
    yield
    la = _log_sigmoid(pre) * (1.0 / GLA_TAU)
    la_c = jnp.concatenate([la[rs, :] for rs in rsl], axis=1)
    la_hi, la_mid, la_lo = _split3(la_c)
    yield
    b_c = _dot(tril, la_hi) + _dot(tril, la_mid) + _dot(tril, la_lo)
    yield

    def scaled(c):
        b = b_c[:, c * GLA_W:(c + 1) * GLA_W]
        b_mid = b[c_len // 2 - 1:c_len // 2, :]
        b_last = b[c_len - 1:c_len, :]
        e_q = jnp.exp(b - b_mid)
        q_in = q_ref[rsl[c], :].astype(F32) * scale * e_q
        k_in = k_ref[rsl[c], :].astype(F32) / e_q
        return dict(
            q_in=q_in.astype(BF16),
            k_in=k_in.astype(BF16),
            q_ex=(q_in * jnp.exp(b_mid)).astype(BF16),
            k_ex=(k_in * jnp.exp(b_last - b_mid)).astype(BF16),
            decay=jnp.exp(b_last),
            v=v_ref[rsl[c], :])

    def scores(s):
        s["attn"] = [_dot_nt(s["q_in"][:, hs], s["k_in"][:, hs]) for hs in hsl]
        s["kv"] = [_dot_tn(s["v"][:, hs], s["k_ex"][:, hs]) for hs in hsl]

    def masked(s):
        s["attn"] = [jnp.where(causal, a, 0.0).astype(BF16) for a in s["attn"]]

    def intra(s):
        s["o"] = [_dot(a, s["v"][:, hs]) for a, hs in zip(s["attn"], hsl)]

    quarters = [list(range(q * n_chunks // 4, (q + 1) * n_chunks // 4)) for q in range(4)]
    parts = {}
    for qt in quarters:
        for c in qt:
            parts[c] = scaled(c)
        yield
        for c in qt:
            scores(parts[c])
    yield
    for qt in quarters:
        for c in qt:
            masked(parts[c])
        yield
        for c in qt:
            intra(parts[c])
    yield

    states = [st_ref[h] for h in heads]
    for c in range(n_chunks):
        s = parts[c]
        inter = [_dot_nt(s["q_ex"][:, hs], states[h].astype(BF16)) for h, hs in enumerate(hsl)]
        states = [states[h] * s["decay"][:, hs] + s["kv"][h] for h, hs in enumerate(hsl)]
        if c % 2 == 0:
            yield
        for h, hs in enumerate(hsl):
            o = s["o"][h] + inter[h]
            o = o * lax.rsqrt(jnp.mean(o * o, axis=-1, keepdims=True) + EPS) * gain[:, hs]
            hr = 0.5 * r_ref[rsl[c], hs].astype(F32)
            o_ref[rsl[c], hs] = (o * (hr + hr * jnp.tanh(hr))).astype(BF16)
    for h in heads:
        st_ref[h] = states[h]


def _swa_stages(seq_start, sink_ref, q_ref, k_ref, v_ref, kp, vp, o_ref):
    w = SWA_WINDOW
    hd = SWA_HEAD_DIM
    grp = SWA_GROUP
    n_blk = q_ref.shape[0] // w
    scale = hd ** -0.5

    row = lax.broadcasted_iota(jnp.int32, (grp * w, 2 * w), 0)
    col = lax.broadcasted_iota(jnp.int32, (grp * w, 2 * w), 1)
    head_in_group = row // w
    dist = w + (row % w) - col
    valid = jnp.logical_and(dist >= 0, dist < w)
    no_prev = jnp.logical_and(seq_start, col < w)
    grp_col = lax.broadcasted_iota(jnp.int32, (grp * w, 1), 0) // w

    bias, bias_first, sinks, keys, vals = [], [], [], [], []
    for h in range(SWA_KV_HEADS):
        slope = jnp.zeros((grp * w, 2 * w), F32)
        sink = jnp.zeros((grp * w, 1), F32)
        for g in range(grp):
            hq = h * grp + g
            slope = jnp.where(head_in_group == g, 2.0 ** (-(8.0 / SWA_Q_HEADS) * (hq + 1)), slope)
            sink = jnp.where(grp_col == g, sink_ref[hq], sink)
        b = jnp.where(valid, -slope * dist.astype(F32), -jnp.inf)
        bias.append(b)
        bias_first.append(jnp.where(no_prev, -jnp.inf, b))
        sinks.append(sink)
        ks = slice(h * hd, (h + 1) * hd)
        keys.append(jnp.concatenate([kp[:, ks], k_ref[:, ks]], axis=0))
        v_h = jnp.concatenate([vp[:, ks], v_ref[:, ks]], axis=0)
        vals.append(jnp.concatenate([v_h, jnp.ones_like(v_h)], axis=1))
    yield

    def scores(h, j):
        qs = [q_ref[j * w:(j + 1) * w, (h * grp + g) * hd:(h * grp + g + 1) * hd] for g in range(grp)]
        q = (jnp.concatenate(qs, axis=0).astype(F32) * scale).astype(BF16)
        return _dot_nt(q, keys[h][j * w:(j + 2) * w, :])

    def finish(h, j, s):
        logits = s + (bias_first[h] if j == 0 else bias[h])
        m = jnp.maximum(jnp.max(logits, axis=-1, keepdims=True), sinks[h])
        p = jnp.exp(logits - m).astype(BF16)
        ov = _dot(p, vals[h][j * w:(j + 2) * w, :])
        o = ov[:, :hd] / (ov[:, hd:] + jnp.exp(sinks[h] - m))
        for g in range(grp):
            hq = h * grp + g
            o_ref[j * w:(j + 1) * w, hq * hd:(hq + 1) * hd] = o[g * w:(g + 1) * w, :].astype(BF16)

    jobs = [(h, j) for h in range(SWA_KV_HEADS) for j in range(n_blk)]
    s_next = scores(*jobs[0])
    for idx, (h, j) in enumerate(jobs):
        s = s_next
        if idx + 1 < len(jobs):
            s_next = scores(*jobs[idx + 1])
        yield
        finish(h, j, s)


def _front_kernel(tiles_per_seq, n_tiles, layer, h_ref, g_ref, w_ref, wga_ref, wa_hi_ref, wa_lo_ref, ba_ref,
                  gn_ref, cw_ref, sink_ref, tg_ref, cv_ref, og_ref, os_ref,
                  zm_ref, ga_scr, kvt_ref, st_ref, ptail_ref):
    i = pl.program_id(0)
    cur = i % 2
    prv = 1 - cur
    n = h_ref.shape[0]
    tile = jnp.minimum(i, n_tiles - 1)

    @pl.when(i == 0)
    def _():
        zm_ref[1] = jnp.zeros(zm_ref.shape[1:], zm_ref.dtype)
        ga_scr[1] = jnp.zeros(ga_scr.shape[1:], ga_scr.dtype)
        kvt_ref[...] = jnp.zeros_like(kvt_ref)
        ptail_ref[...] = jnp.zeros_like(ptail_ref)

    u = _rms(h_ref[...], g_ref[...]).astype(BF16)
    seq_start = jnp.maximum(i - 1, 0) % tiles_per_seq == 0
    zprev = lambda off, width: zm_ref.at[prv, :, off:off + width]
    kv_tail = kvt_ref[...]
    mixers = [
        _gla_stages(seq_start, zprev(M_Q, GLA_W), zprev(M_K, GLA_W), zprev(M_V, GLA_W), zprev(M_R, GLA_W),
                    ga_scr.at[prv], wa_hi_ref, wa_lo_ref, ba_ref, gn_ref, og_ref, st_ref),
        _swa_stages(seq_start, sink_ref.at[layer], zprev(M_SQ, SWA_QW), zprev(M_SK, SWA_KW), zprev(M_SV, SWA_KW),
                    kv_tail[:, :SWA_KW], kv_tail[:, SWA_KW:], os_ref),
    ]

    def advance():
        for gen in list(mixers):
            try:
                next(gen)
            except StopIteration:
                mixers.remove(gen)

    held = {}
    for off in range(0, NZ, PROJ_CHUNK):
        wd = min(PROJ_CHUNK, NZ - off)
        res = _dot_nt(u, w_ref[off:off + wd, :])
        if off < N_GATES:
            tg_ref[:, off:off + wd] = jnp.tanh(0.5 * res).astype(BF16)
        elif off == Z_CC:
            held["cc"] = res
        elif off == Z_CX:
            prod = held.pop("cc") * res
            prev = jnp.where(tile % tiles_per_seq == 0, 0.0, ptail_ref[(tile + 1) % 2])
            ptail_ref[tile % 2] = prod[n - SUBLANE:, :]
            held["conv"] = _causal_conv3(prod, prev, cw_ref[...])
        elif off == Z_CB:
            cv_ref[...] = (res * held.pop("conv")).astype(BF16)
        else:
            zm_ref[cur, :, off - ZC:off - ZC + wd] = res.astype(BF16)
        advance()
    ga_scr[cur] = _dot_nt(u, wga_ref[...])
    while mixers:
        advance()
    kvt_ref[...] = zm_ref[prv, n - SWA_WINDOW:, M_SK:M_SK + 2 * SWA_KW]


def _front(layer, h, g, w_main, w_ga, wa_hi, wa_lo, b_alpha, gain, conv_w, sinks, seq_len):
    t = h.shape[0]
    tm = TOKEN_TILE
    n_tiles = t // tm
    cur_blk = lambda i: (jnp.minimum(i, n_tiles - 1), 0)
    prev_blk = lambda i: (jnp.maximum(i - 1, 0), 0)
    return pl.pallas_call(
        functools.partial(_front_kernel, seq_len // tm, n_tiles, layer),
        grid=(n_tiles + 1,),
        in_specs=[
            pl.BlockSpec((tm, D_MODEL), cur_blk),
            _resident((1, D_MODEL), layer),
            _resident((NZ, D_MODEL), layer),
            _resident((LANE, D_MODEL), layer),
            _resident((LANE, GLA_W), layer), _resident((LANE, GLA_W), layer),
            _resident((1, GLA_W), layer), _resident((1, GLA_W), layer),
            _resident((3, CONV_CH), layer),
            pl.BlockSpec(memory_space=pltpu.SMEM),
        ],
        out_specs=[
            pl.BlockSpec((tm, N_GATES), cur_blk),
            pl.BlockSpec((tm, CONV_CH), cur_blk),
            pl.BlockSpec((tm, GLA_W), prev_blk),
            pl.BlockSpec((tm, SWA_QW), prev_blk),
        ],
        out_shape=[
            jax.ShapeDtypeStruct((t, N_GATES), BF16),
            jax.ShapeDtypeStruct((t, CONV_CH), BF16),
            jax.ShapeDtypeStruct((t, GLA_W), BF16),
            jax.ShapeDtypeStruct((t, SWA_QW), BF16),
        ],
        scratch_shapes=[pltpu.VMEM((2, tm, ZM), BF16),
                        pltpu.VMEM((2, tm, LANE), F32),
                        pltpu.VMEM((SWA_WINDOW, 2 * SWA_KW), BF16),
                        pltpu.VMEM((GLA_HEADS, GLA_DV, GLA_DK), F32),
                        pltpu.VMEM((2, SUBLANE, CONV_CH), F32)],
        compiler_params=_params(1),
        name="front",
    )(h, g, w_main, w_ga, wa_hi, wa_lo, b_alpha, gain, conv_w, sinks)


def _merge_kernel(h_ref, og_ref, os_ref, cv_ref, t1_ref, t2_ref, t3_ref,
                  wg_ref, wc_ref, ws_ref, wo_ref, o_ref):
    y_gla = _dot(og_ref[...], wg_ref[...])
    y_conv = _dot(cv_ref[...], wc_ref[...])
    y_swa = _dot(os_ref[...], ws_ref[...])

    def gate(t_ref, y):
        return y + t_ref[...].astype(F32) * y

    merged = 0.5 * (gate(t1_ref, y_gla) + gate(t2_ref, y_conv) + gate(t3_ref, y_swa))
    o_ref[...] = h_ref[...] + _dot(merged.astype(BF16), wo_ref[...])


def _merge(layer, h, o_gla, o_swa, cv, tg, w_gla_o, w_conv_o, w_swa_o, w_o):
    t = h.shape[0]
    tm = MERGE_TILE
    row = lambda width, j=0: pl.BlockSpec((tm, width), lambda i: (i, j))
    return pl.pallas_call(
        _merge_kernel,
        grid=(t // tm,),
        in_specs=[
            row(D_MODEL), row(GLA_W), row(SWA_QW), row(CONV_CH),
            row(D_MODEL, 0), row(D_MODEL, 1), row(D_MODEL, 2),
            _resident((GLA_W, D_MODEL), layer),
            _resident((CONV_CH, D_MODEL), layer),
            _resident((SWA_QW, D_MODEL), layer),
            _resident((D_MODEL, D_MODEL), layer),
        ],
        out_specs=row(D_MODEL),
        out_shape=jax.ShapeDtypeStruct((t, D_MODEL), F32),
        compiler_params=_params(1),
        name="merge",
    )(h, o_gla, o_swa, cv, tg, tg, tg, w_gla_o, w_conv_o, w_swa_o, w_o)


def _ffn_kernel(tiles_per_seq, final_norm, h_ref, g_ref, wup_ref, cw_ref, wd_ref, gf_ref, o_ref,
                act_ref, tail_ref):
    @pl.when(pl.program_id(0) % tiles_per_seq == 0)
    def _():
        tail_ref[...] = jnp.zeros_like(tail_ref)

    x = h_ref[...]
    u = _rms(x, g_ref[...]).astype(BF16)
    n = x.shape[0]
    nf = -(-D_FF // FF_TILE)
    gcol = lambda c: slice(c * FF_TILE, min((c + 1) * FF_TILE, D_FF))
    vcol = lambda c: slice(D_FF + c * FF_TILE, D_FF + min((c + 1) * FF_TILE, D_FF))

    def up(c):
        return _dot(u, wup_ref[:, gcol(c)]), _dot(u, wup_ref[:, vcol(c)])

    def gated(c, hid):
        hg, hv = hid
        pg = tail_ref[0, :, gcol(c)]
        pv = tail_ref[1, :, gcol(c)]
        tail_ref[0, :, gcol(c)] = hg[n - SUBLANE:, :]
        tail_ref[1, :, gcol(c)] = hv[n - SUBLANE:, :]
        s = _causal_conv3(hg, pg, 0.5 * cw_ref[:, gcol(c)])
        v = _causal_conv3(hv, pv, cw_ref[:, vcol(c)])
        act_ref[:, gcol(c)] = ((s + s * jnp.tanh(s)) * v).astype(BF16)

    hid = up(0)
    for c in range(nf):
        nxt = up(c + 1) if c + 1 < nf else None
        gated(c, hid)
        hid = nxt
    out = x + _dot(act_ref[...], wd_ref[...])
    if final_norm:
        out = _rms(out, gf_ref[...])
    o_ref[...] = out


def _ffn(layer, h, g, wup, cw, wd, g_final, seq_len, final_norm):
    t = h.shape[0]
    tm = TOKEN_TILE
    return pl.pallas_call(
        functools.partial(_ffn_kernel, seq_len // tm, final_norm),
        grid=(t // tm,),
        in_specs=[
            pl.BlockSpec((tm, D_MODEL), lambda i: (i, 0)),
            _resident((1, D_MODEL), layer),
            _resident((D_MODEL, 2 * D_FF), layer),
            _resident((3, 2 * D_FF), layer),
            _resident((D_FF, D_MODEL), layer),
            _resident((1, D_MODEL)),
        ],
        out_specs=pl.BlockSpec((tm, D_MODEL), lambda i: (i, 0)),
        out_shape=jax.ShapeDtypeStruct((t, D_MODEL), F32),
        scratch_shapes=[pltpu.VMEM((tm, D_FF), BF16),
                        pltpu.VMEM((2, SUBLANE, D_FF), F32)],
        compiler_params=_params(1),
        name="ffn",
    )(h, g, wup, cw, wd, g_final)


_W_PIECES = (
    (_SRC_GATES, 3 * D_MODEL),
    (_SRC_CONV + 2 * CONV_CH, CONV_CH),
    (_SRC_CONV, CONV_CH),
    (_SRC_CONV + CONV_CH, CONV_CH),
    (_SRC_GLA, 4 * GLA_W),
    (_SRC_SWA, SWA_QW + 2 * SWA_KW),
)
W_COLS = 256


def _relayout_kernel(wt_ref, wm_ref, wga_ref):
    off = 0
    for src, width in _W_PIECES:
        wm_ref[off:off + width, :] = wt_ref[src:src + width, :].astype(BF16)
        off += width
    wga_ref[0:GLA_RANK, :] = wt_ref[_SRC_GA:_SRC_GA + GLA_RANK, :].astype(BF16)
    wga_ref[GLA_RANK:, :] = jnp.zeros((LANE - GLA_RANK, wga_ref.shape[1]), BF16)


def _relayout(w_t):
    depth, n, k = w_t.shape
    return pl.pallas_call(
        _relayout_kernel,
        grid=(depth, k // W_COLS),
        in_specs=[pl.BlockSpec((None, n, W_COLS), lambda l, i: (l, 0, i))],
        out_specs=[pl.BlockSpec((None, NZ, W_COLS), lambda l, i: (l, 0, i)),
                   pl.BlockSpec((None, LANE, W_COLS), lambda l, i: (l, 0, i))],
        out_shape=[jax.ShapeDtypeStruct((depth, NZ, k), BF16), jax.ShapeDtypeStruct((depth, LANE, k), BF16)],
        compiler_params=_params(2),
        name="relayout",
    )(w_t)


def _prep_weights(w_in, w_alpha):
    w_main, w_ga = _relayout(jnp.swapaxes(w_in, 1, 2))
    wa = jnp.pad(w_alpha, ((0, 0), (0, LANE - GLA_RANK), (0, 0)))
    wa_hi = wa.astype(BF16)
    wa_lo = (wa - wa_hi.astype(F32)).astype(BF16)
    return w_main, w_ga, wa_hi, wa_lo


def kernel(x, g_mix, w_in, gla_w_alpha, gla_b_alpha, gla_norm_g, conv_w, swa_sinks, w_gla_o,
           w_conv_o, w_swa_o, w_o, g_ffn, w_up, ffn_conv_w, w_down, g_final):
    bsz, seq_len, d = x.shape
    assert d == D_MODEL and seq_len % TOKEN_TILE == 0 and (bsz * seq_len) % MERGE_TILE == 0
    depth = w_in.shape[0]
    w_main, w_ga, wa_hi, wa_lo = _prep_weights(w_in, gla_w_alpha)
    row = lambda p: p[:, None, :]
    g_mix, b_alpha, gain, g_ffn = row(g_mix), row(gla_b_alpha), row(gla_norm_g), row(g_ffn)
    w_gla_o, w_conv_o, w_swa_o, w_o = (w.astype(BF16) for w in (w_gla_o, w_conv_o, w_swa_o, w_o))
    w_up, w_down = w_up.astype(BF16), w_down.astype(BF16)
    h = x.reshape(bsz * seq_len, d)
    for l in range(depth):
        tg, cv, o_gla, o_swa = _front(l, h, g_mix, w_main, w_ga, wa_hi, wa_lo, b_alpha, gain, conv_w,
                                      swa_sinks, seq_len)
        h = _merge(l, h, o_gla, o_swa, cv, tg, w_gla_o, w_conv_o, w_swa_o, w_o)
        h = _ffn(l, h, g_ffn, w_up, ffn_conv_w, w_down, g_final[None, :], seq_len, final_norm=(l == depth - 1))
    return h.reshape(bsz, seq_len, d)
```

```python
import functools

import jax
import jax.numpy as jnp
from jax import lax
from jax.experimental import pallas as pl
from jax.experimental.pallas import tpu as pltpu

F32 = jnp.float32
BF16 = jnp.bfloat16

D_MODEL = 1024
GLA_HEADS = 4
GLA_DK = 128
GLA_DV = 128
GLA_RANK = 16
GLA_TAU = 16.0
GLA_CHUNK = 64
CONV_CH = D_MODEL // 2
SWA_Q_HEADS = 8
SWA_KV_HEADS = 2
SWA_GROUP = SWA_Q_HEADS // SWA_KV_HEADS
SWA_HEAD_DIM = 64
SWA_WINDOW = 128
D_FF = 2816
EPS = 1e-6

GLA_W = GLA_HEADS * GLA_DK
SWA_QW = SWA_Q_HEADS * SWA_HEAD_DIM
SWA_KW = SWA_KV_HEADS * SWA_HEAD_DIM

_SRC_GLA = 0
_SRC_GA = 4 * GLA_W
_SRC_CONV = _SRC_GA + GLA_RANK
_SRC_SWA = _SRC_CONV + 3 * CONV_CH
_SRC_GATES = _SRC_SWA + SWA_QW + 2 * SWA_KW
_SRC_END = _SRC_GATES + 3 * D_MODEL

Z_GATES = 0
N_GATES = 3 * D_MODEL
Z_CC = N_GATES
Z_CX = Z_CC + CONV_CH
Z_CB = Z_CX + CONV_CH
ZC = Z_CB + CONV_CH
M_Q, M_K, M_V, M_R = 0, GLA_W, 2 * GLA_W, 3 * GLA_W
M_SQ = 4 * GLA_W
M_SK = M_SQ + SWA_QW
M_SV = M_SK + SWA_KW
ZM = M_SV + SWA_KW
NZ = ZC + ZM

LANE = 128
SUBLANE = 8
TOKEN_TILE = 512
MERGE_TILE = 1024
FF_TILE = 512
PROJ_CHUNK = 512
VMEM_LIMIT = 56 * 1024 * 1024


def _rms(x, g):
    return x * lax.rsqrt(jnp.mean(x * x, axis=-1, keepdims=True) + EPS) * g


def _dot(a, b):
    return jnp.dot(a, b, preferred_element_type=F32)


def _dot_nt(a, b):
    return lax.dot_general(a, b, (((1,), (1,)), ((), ())), preferred_element_type=F32)


def _dot_tn(a, b):
    return lax.dot_general(a, b, (((0,), (0,)), ((), ())), preferred_element_type=F32)


def _log_sigmoid(x):
    return jnp.minimum(x, 0.0) - jnp.log(1.0 + jnp.exp(-jnp.abs(x)))


def _split3(x):
    hi = x.astype(BF16)
    r1 = x - hi.astype(F32)
    mid = r1.astype(BF16)
    lo = (r1 - mid.astype(F32)).astype(BF16)
    return hi, mid, lo


def _resident(shape, layer=None):
    nd = len(shape)
    if layer is None:
        return pl.BlockSpec(shape, lambda *_: (0,) * nd, pipeline_mode=pl.Buffered(1))
    return pl.BlockSpec((None,) + tuple(shape), lambda *_: (layer,) + (0,) * nd, pipeline_mode=pl.Buffered(1))


def _params(n_axes):
    return pltpu.CompilerParams(
        dimension_semantics=("arbitrary",) * n_axes, vmem_limit_bytes=VMEM_LIMIT)


def _causal_conv3(y, prev, w):
    n = y.shape[0]
    ext = jnp.concatenate([prev, y], axis=0)
    y1 = ext[SUBLANE - 1:SUBLANE - 1 + n, :]
    y2 = ext[SUBLANE - 2:SUBLANE - 2 + n, :]
    return w[0:1, :] * y2 + w[1:2, :] * y1 + w[2:3, :] * y


def _gla_stages(seq_start, q_ref, k_ref, v_ref, r_ref, ga_ref, wa_hi_ref, wa_lo_ref, ba_ref,
                g_ref, o_ref, st_ref):
    c_len = GLA_CHUNK
    n_chunks = o_ref.shape[0] // c_len

    @pl.when(seq_start)
    def _():
        st_ref[...] = jnp.zeros_like(st_ref)

    row = lax.broadcasted_iota(jnp.int32, (c_len, c_len), 0)
    col = lax.broadcasted_iota(jnp.int32, (c_len, c_len), 1)
    causal = row >= col
    tril = jnp.where(causal, 1.0, 0.0).astype(BF16)
    scale = GLA_DK ** -0.5
    gain = g_ref[...]
    heads = range(GLA_HEADS)
    hsl = [slice(h * GLA_DK, (h + 1) * GLA_DK) for h in heads]
    rsl = [slice(c * c_len, (c + 1) * c_len) for c in range(n_chunks)]

    ga = ga_ref[...]
    ga_hi = ga.astype(BF16)
    ga_lo = (ga - ga_hi.astype(F32)).astype(BF16)
    pre = (_dot(ga_hi, wa_hi_ref[...]) + _dot(ga_lo, wa_hi_ref[...])
           + _dot(ga_hi, wa_lo_ref[...]) + ba_ref[...])
    yield
    la = _log_sigmoid(pre) * (1.0 / GLA_TAU)
    la_c = jnp.concatenate([la[rs, :] for rs in rsl], axis=1)
    la_hi, la_mid, la_lo = _split3(la_c)
    yield
    b_c = _dot(tril, la_hi) + _dot(tril, la_mid) + _dot(tril, la_lo)
    yield

    def scaled(c):
        b = b_c[:, c * GLA_W:(c + 1) * GLA_W]
        b_mid = b[c_len // 2 - 1:c_len // 2, :]
        b_last = b[c_len - 1:c_len, :]
        e_q = jnp.exp(b - b_mid)
        q_in = q_ref[rsl[c], :].astype(F32) * scale * e_q
        k_in = k_ref[rsl[c], :].astype(F32) / e_q
        return dict(
            q_in=q_in.astype(BF16),
            k_in=k_in.astype(BF16),
            q_ex=(q_in * jnp.exp(b_mid)).astype(BF16),
            k_ex=(k_in * jnp.exp(b_last - b_mid)).astype(BF16),
            decay=jnp.exp(b_last),
            v=v_ref[rsl[c], :])

    def scores(s):
        s["attn"] = [_dot_nt(s["q_in"][:, hs], s["k_in"][:, hs]) for hs in hsl]
        s["kv"] = [_dot_tn(s["v"][:, hs], s["k_ex"][:, hs]) for hs in hsl]

    def masked(s):
        s["attn"] = [jnp.where(causal, a, 0.0).astype(BF16) for a in s["attn"]]

    def intra(s):
        s["o"] = [_dot(a, s["v"][:, hs]) for a, hs in zip(s["attn"], hsl)]

    quarters = [list(range(q * n_chunks // 4, (q + 1) * n_chunks // 4)) for q in range(4)]
    parts = {}
    for qt in quarters:
        for c in qt:
            parts[c] = scaled(c)
        yield
        for c in qt:
            scores(parts[c])
    yield
    for qt in quarters:
        for c in qt:
            masked(parts[c])
        yield
        for c in qt:
            intra(parts[c])
    yield

    states = [st_ref[h] for h in heads]
    for c in range(n_chunks):
        s = parts[c]
        inter = [_dot_nt(s["q_ex"][:, hs], states[h].astype(BF16)) for h, hs in enumerate(hsl)]
        states = [states[h] * s["decay"][:, hs] + s["kv"][h] for h, hs in enumerate(hsl)]
        if c % 2 == 0:
            yield
        for h, hs in enumerate(hsl):
            o = s["o"][h] + inter[h]
            o = o * lax.rsqrt(jnp.mean(o * o, axis=-1, keepdims=True) + EPS) * gain[:, hs]
            hr = 0.5 * r_ref[rsl[c], hs].astype(F32)
            o_ref[rsl[c], hs] = (o * (hr + hr * jnp.tanh(hr))).astype(BF16)
    for h in heads:
        st_ref[h] = states[h]


def _swa_stages(seq_start, sink_ref, q_ref, k_ref, v_ref, kp, vp, o_ref):
    w = SWA_WINDOW
    hd = SWA_HEAD_DIM
    grp = SWA_GROUP
    n_blk = q_ref.shape[0] // w
    scale = hd ** -0.5

    row = lax.broadcasted_iota(jnp.int32, (grp * w, 2 * w), 0)
    col = lax.broadcasted_iota(jnp.int32, (grp * w, 2 * w), 1)
    head_in_group = row // w
    dist = w + (row % w) - col
    valid = jnp.logical_and(dist >= 0, dist < w)
    no_prev = jnp.logical_and(seq_start, col < w)
    grp_col = lax.broadcasted_iota(jnp.int32, (grp * w, 1), 0) // w

    bias, bias_first, sinks, keys, vals = [], [], [], [], []
    for h in range(SWA_KV_HEADS):
        slope = jnp.zeros((grp * w, 2 * w), F32)
        sink = jnp.zeros((grp * w, 1), F32)
        for g in range(grp):
            hq = h * grp + g
            slope = jnp.where(head_in_group == g, 2.0 ** (-(8.0 / SWA_Q_HEADS) * (hq + 1)), slope)
            sink = jnp.where(grp_col == g, sink_ref[hq], sink)
        b = jnp.where(valid, -slope * dist.astype(F32), -jnp.inf)
        bias.append(b)
        bias_first.append(jnp.where(no_prev, -jnp.inf, b))
        sinks.append(sink)
        ks = slice(h * hd, (h + 1) * hd)
        keys.append(jnp.concatenate([kp[:, ks], k_ref[:, ks]], axis=0))
        v_h = jnp.concatenate([vp[:, ks], v_ref[:, ks]], axis=0)
        vals.append(jnp.concatenate([v_h, jnp.ones_like(v_h)], axis=1))
    yield

    def scores(h, j):
        qs = [q_ref[j * w:(j + 1) * w, (h * grp + g) * hd:(h * grp + g + 1) * hd] for g in range(grp)]
        q = (jnp.concatenate(qs, axis=0).astype(F32) * scale).astype(BF16)
        return _dot_nt(q, keys[h][j * w:(j + 2) * w, :])

    def finish(h, j, s):
        logits = s + (bias_first[h] if j == 0 else bias[h])
        m = jnp.maximum(jnp.max(logits, axis=-1, keepdims=True), sinks[h])
        p = jnp.exp(logits - m).astype(BF16)
        ov = _dot(p, vals[h][j * w:(j + 2) * w, :])
        o = ov[:, :hd] / (ov[:, hd:] + jnp.exp(sinks[h] - m))
        for g in range(grp):
            hq = h * grp + g
            o_ref[j * w:(j + 1) * w, hq * hd:(hq + 1) * hd] = o[g * w:(g + 1) * w, :].astype(BF16)

    jobs = [(h, j) for h in range(SWA_KV_HEADS) for j in range(n_blk)]
    s_next = scores(*jobs[0])
    for idx, (h, j) in enumerate(jobs):
        s = s_next
        if idx + 1 < len(jobs):
            s_next = scores(*jobs[idx + 1])
        yield
        finish(h, j, s)


def _front_kernel(tiles_per_seq, n_tiles, layer, h_ref, g_ref, w_ref, wga_ref, wa_hi_ref, wa_lo_ref, ba_ref,
                  gn_ref, cw_ref, sink_ref, tg_ref, cv_ref, og_ref, os_ref,
                  zm_ref, ga_scr, kvt_ref, st_ref, ptail_ref):
    i = pl.program_id(0)
    cur = i % 2
    prv = 1 - cur
    n = h_ref.shape[0]
    tile = jnp.minimum(i, n_tiles - 1)

    @pl.when(i == 0)
    def _():
        zm_ref[1] = jnp.zeros(zm_ref.shape[1:], zm_ref.dtype)
        ga_scr[1] = jnp.zeros(ga_scr.shape[1:], ga_scr.dtype)
        kvt_ref[...] = jnp.zeros_like(kvt_ref)
        ptail_ref[...] = jnp.zeros_like(ptail_ref)

    u = _rms(h_ref[...], g_ref[...]).astype(BF16)
    seq_start = jnp.maximum(i - 1, 0) % tiles_per_seq == 0
    zprev = lambda off, width: zm_ref.at[prv, :, off:off + width]
    kv_tail = kvt_ref[...]
    mixers = [
        _gla_stages(seq_start, zprev(M_Q, GLA_W), zprev(M_K, GLA_W), zprev(M_V, GLA_W), zprev(M_R, GLA_W),
                    ga_scr.at[prv], wa_hi_ref, wa_lo_ref, ba_ref, gn_ref, og_ref, st_ref),
        _swa_stages(seq_start, sink_ref.at[layer], zprev(M_SQ, SWA_QW), zprev(M_SK, SWA_KW), zprev(M_SV, SWA_KW),
                    kv_tail[:, :SWA_KW], kv_tail[:, SWA_KW:], os_ref),
    ]

    def advance():
        for gen in list(mixers):
            try:
                next(gen)
            except StopIteration:
                mixers.remove(gen)

    held = {}
    for off in range(0, NZ, PROJ_CHUNK):
        wd = min(PROJ_CHUNK, NZ - off)
        res = _dot(u, w_ref[:, off:off + wd])
        if off < N_GATES:
            tg_ref[:, off:off + wd] = jnp.tanh(0.5 * res).astype(BF16)
        elif off == Z_CC:
            held["cc"] = res
        elif off == Z_CX:
            prod = held.pop("cc") * res
            prev = jnp.where(tile % tiles_per_seq == 0, 0.0, ptail_ref[(tile + 1) % 2])
            ptail_ref[tile % 2] = prod[n - SUBLANE:, :]
            held["conv"] = _causal_conv3(prod, prev, cw_ref[...])
        elif off == Z_CB:
            cv_ref[...] = (res * held.pop("conv")).astype(BF16)
        else:
            zm_ref[cur, :, off - ZC:off - ZC + wd] = res.astype(BF16)
        advance()
    ga_scr[cur] = _dot(u, wga_ref[...])
    while mixers:
        advance()
    kvt_ref[...] = zm_ref[prv, n - SWA_WINDOW:, M_SK:M_SK + 2 * SWA_KW]


def _front(layer, h, g, w_main, w_ga, wa_hi, wa_lo, b_alpha, gain, conv_w, sinks, seq_len):
    t = h.shape[0]
    tm = TOKEN_TILE
    n_tiles = t // tm
    cur_blk = lambda i: (jnp.minimum(i, n_tiles - 1), 0)
    prev_blk = lambda i: (jnp.maximum(i - 1, 0), 0)
    return pl.pallas_call(
        functools.partial(_front_kernel, seq_len // tm, n_tiles, layer),
        grid=(n_tiles + 1,),
        in_specs=[
            pl.BlockSpec((tm, D_MODEL), cur_blk),
            _resident((1, D_MODEL), layer),
            _resident((D_MODEL, NZ), layer),
            _resident((D_MODEL, LANE), layer),
            _resident((LANE, GLA_W), layer), _resident((LANE, GLA_W), layer),
            _resident((1, GLA_W), layer), _resident((1, GLA_W), layer),
            _resident((3, CONV_CH), layer),
            pl.BlockSpec(memory_space=pltpu.SMEM),
        ],
        out_specs=[
            pl.BlockSpec((tm, N_GATES), cur_blk),
            pl.BlockSpec((tm, CONV_CH), cur_blk),
            pl.BlockSpec((tm, GLA_W), prev_blk),
            pl.BlockSpec((tm, SWA_QW), prev_blk),
        ],
        out_shape=[
            jax.ShapeDtypeStruct((t, N_GATES), BF16),
            jax.ShapeDtypeStruct((t, CONV_CH), BF16),
            jax.ShapeDtypeStruct((t, GLA_W), BF16),
            jax.ShapeDtypeStruct((t, SWA_QW), BF16),
        ],
        scratch_shapes=[pltpu.VMEM((2, tm, ZM), BF16),
                        pltpu.VMEM((2, tm, LANE), F32),
                        pltpu.VMEM((SWA_WINDOW, 2 * SWA_KW), BF16),
                        pltpu.VMEM((GLA_HEADS, GLA_DV, GLA_DK), F32),
                        pltpu.VMEM((2, SUBLANE, CONV_CH), F32)],
        compiler_params=_params(1),
        name="front",
    )(h, g, w_main, w_ga, wa_hi, wa_lo, b_alpha, gain, conv_w, sinks)


def _merge_kernel(h_ref, og_ref, os_ref, cv_ref, t1_ref, t2_ref, t3_ref,
                  wg_ref, wc_ref, ws_ref, wo_ref, o_ref):
    y_gla = _dot(og_ref[...], wg_ref[...])
    y_conv = _dot(cv_ref[...], wc_ref[...])
    y_swa = _dot(os_ref[...], ws_ref[...])

    def gate(t_ref, y):
        return y + t_ref[...].astype(F32) * y

    merged = 0.5 * (gate(t1_ref, y_gla) + gate(t2_ref, y_conv) + gate(t3_ref, y_swa))
    o_ref[...] = h_ref[...] + _dot(merged.astype(BF16), wo_ref[...])


def _merge(layer, h, o_gla, o_swa, cv, tg, w_gla_o, w_conv_o, w_swa_o, w_o):
    t = h.shape[0]
    tm = MERGE_TILE
    row = lambda width, j=0: pl.BlockSpec((tm, width), lambda i: (i, j))
    return pl.pallas_call(
        _merge_kernel,
        grid=(t // tm,),
        in_specs=[
            row(D_MODEL), row(GLA_W), row(SWA_QW), row(CONV_CH),
            row(D_MODEL, 0), row(D_MODEL, 1), row(D_MODEL, 2),
            _resident((GLA_W, D_MODEL), layer),
            _resident((CONV_CH, D_MODEL), layer),
            _resident((SWA_QW, D_MODEL), layer),
            _resident((D_MODEL, D_MODEL), layer),
        ],
        out_specs=row(D_MODEL),
        out_shape=jax.ShapeDtypeStruct((t, D_MODEL), F32),
        compiler_params=_params(1),
        name="merge",
    )(h, o_gla, o_swa, cv, tg, tg, tg, w_gla_o, w_conv_o, w_swa_o, w_o)


def _ffn_kernel(tiles_per_seq, final_norm, h_ref, g_ref, wup_ref, cw_ref, wd_ref, gf_ref, o_ref,
                act_ref, tail_ref):
    @pl.when(pl.program_id(0) % tiles_per_seq == 0)
    def _():
        tail_ref[...] = jnp.zeros_like(tail_ref)

    x = h_ref[...]
    u = _rms(x, g_ref[...]).astype(BF16)
    n = x.shape[0]
    nf = -(-D_FF // FF_TILE)
    gcol = lambda c: slice(c * FF_TILE, min((c + 1) * FF_TILE, D_FF))
    vcol = lambda c: slice(D_FF + c * FF_TILE, D_FF + min((c + 1) * FF_TILE, D_FF))

    def up(c):
        return _dot(u, wup_ref[:, gcol(c)]), _dot(u, wup_ref[:, vcol(c)])

    def gated(c, hid):
        hg, hv = hid
        pg = tail_ref[0, :, gcol(c)]
        pv = tail_ref[1, :, gcol(c)]
        tail_ref[0, :, gcol(c)] = hg[n - SUBLANE:, :]
        tail_ref[1, :, gcol(c)] = hv[n - SUBLANE:, :]
        s = _causal_conv3(hg, pg, 0.5 * cw_ref[:, gcol(c)])
        v = _causal_conv3(hv, pv, cw_ref[:, vcol(c)])
        act_ref[:, gcol(c)] = ((s + s * jnp.tanh(s)) * v).astype(BF16)

    hid = up(0)
    for c in range(nf):
        nxt = up(c + 1) if c + 1 < nf else None
        gated(c, hid)
        hid = nxt
    out = x + _dot(act_ref[...], wd_ref[...])
    if final_norm:
        out = _rms(out, gf_ref[...])
    o_ref[...] = out


def _ffn(layer, h, g, wup, cw, wd, g_final, seq_len, final_norm):
    t = h.shape[0]
    tm = TOKEN_TILE
    return pl.pallas_call(
        functools.partial(_ffn_kernel, seq_len // tm, final_norm),
        grid=(t // tm,),
        in_specs=[
            pl.BlockSpec((tm, D_MODEL), lambda i: (i, 0)),
            _resident((1, D_MODEL), layer),
            _resident((D_MODEL, 2 * D_FF), layer),
            _resident((3, 2 * D_FF), layer),
            _resident((D_FF, D_MODEL), layer),
            _resident((1, D_MODEL)),
        ],
        out_specs=pl.BlockSpec((tm, D_MODEL), lambda i: (i, 0)),
        out_shape=jax.ShapeDtypeStruct((t, D_MODEL), F32),
        scratch_shapes=[pltpu.VMEM((tm, D_FF), BF16),
                        pltpu.VMEM((2, SUBLANE, D_FF), F32)],
        compiler_params=_params(1),
        name="ffn",
    )(h, g, wup, cw, wd, g_final)


_W_PIECES = (
    (_SRC_GATES, 3 * D_MODEL),
    (_SRC_CONV + 2 * CONV_CH, CONV_CH),
    (_SRC_CONV, CONV_CH),
    (_SRC_CONV + CONV_CH, CONV_CH),
    (_SRC_GLA, 4 * GLA_W),
    (_SRC_SWA, SWA_QW + 2 * SWA_KW),
)
W_COLS = 256


def _relayout_kernel(wt_ref, wm_ref, wga_ref):
    off = 0
    for src, width in _W_PIECES:
        wm_ref[:, off:off + width] = wt_ref[src:src + width, :].T.astype(BF16)
        off += width
    lane = lax.broadcasted_iota(jnp.int32, wga_ref.shape, 1)
    wga_ref[...] = jnp.where(lane < GLA_RANK, wt_ref[_SRC_GA:_SRC_GA + LANE, :].T, 0.0).astype(BF16)


def _relayout(w_t):
    depth, n, k = w_t.shape
    return pl.pallas_call(
        _relayout_kernel,
        grid=(depth, k // W_COLS),
        in_specs=[pl.BlockSpec((None, n, W_COLS), lambda l, i: (l, 0, i))],
        out_specs=[pl.BlockSpec((None, W_COLS, NZ), lambda l, i: (l, i, 0)),
                   pl.BlockSpec((None, W_COLS, LANE), lambda l, i: (l, i, 0))],
        out_shape=[jax.ShapeDtypeStruct((depth, k, NZ), BF16), jax.ShapeDtypeStruct((depth, k, LANE), BF16)],
        compiler_params=_params(2),
        name="relayout",
    )(w_t)


def _prep_weights(w_in, w_alpha):
    w_main, w_ga = _relayout(jnp.swapaxes(w_in, 1, 2))
    wa = jnp.pad(w_alpha, ((0, 0), (0, LANE - GLA_RANK), (0, 0)))
    wa_hi = wa.astype(BF16)
    wa_lo = (wa - wa_hi.astype(F32)).astype(BF16)
    return w_main, w_ga, wa_hi, wa_lo


def kernel(x, g_mix, w_in, gla_w_alpha, gla_b_alpha, gla_norm_g, conv_w, swa_sinks, w_gla_o,
           w_conv_o, w_swa_o, w_o, g_ffn, w_up, ffn_conv_w, w_down, g_final):
    bsz, seq_len, d = x.shape
    assert d == D_MODEL and seq_len % TOKEN_TILE == 0 and (bsz * seq_len) % MERGE_TILE == 0
    depth = w_in.shape[0]
    w_main, w_ga, wa_hi, wa_lo = _prep_weights(w_in, gla_w_alpha)
    row = lambda p: p[:, None, :]
    g_mix, b_alpha, gain, g_ffn = row(g_mix), row(gla_b_alpha), row(gla_norm_g), row(g_ffn)
    w_gla_o, w_conv_o, w_swa_o, w_o = (w.astype(BF16) for w in (w_gla_o, w_conv_o, w_swa_o, w_o))
    w_up, w_down = w_up.astype(BF16), w_down.astype(BF16)
    h = x.reshape(bsz * seq_len, d)
    for l in range(depth):
        tg, cv, o_gla, o_swa = _front(l, h, g_mix, w_main, w_ga, wa_hi, wa_lo, b_alpha, gain, conv_w,
                                      swa_sinks, seq_len)
        h = _merge(l, h, o_gla, o_swa, cv, tg, w_gla_o, w_conv_o, w_swa_o, w_o)
        h = _ffn(l, h, g_ffn, w_up, ffn_conv_w, w_down, g_final[None, :], seq_len, final_norm=(l == depth - 1))
    return h.reshape(bsz, seq_len, d)
```

```python
import functools

import jax
import jax.numpy as jnp
from jax import lax
from jax.experimental import pallas as pl
from jax.experimental.pallas import tpu as pltpu

F32 = jnp.float32
BF16 = jnp.bfloat16

D_MODEL = 1024
GLA_HEADS = 4
GLA_DK = 128
GLA_DV = 128
GLA_RANK = 16
GLA_TAU = 16.0
GLA_CHUNK = 64
CONV_CH = D_MODEL // 2
SWA_Q_HEADS = 8
SWA_KV_HEADS = 2
SWA_GROUP = SWA_Q_HEADS // SWA_KV_HEADS
SWA_HEAD_DIM = 64
SWA_WINDOW = 128
D_FF = 2816
EPS = 1e-6

GLA_W = GLA_HEADS * GLA_DK
SWA_QW = SWA_Q_HEADS * SWA_HEAD_DIM
SWA_KW = SWA_KV_HEADS * SWA_HEAD_DIM

_SRC_GLA = 0
_SRC_GA = 4 * GLA_W
_SRC_CONV = _SRC_GA + GLA_RANK
_SRC_SWA = _SRC_CONV + 3 * CONV_CH
_SRC_GATES = _SRC_SWA + SWA_QW + 2 * SWA_KW
_SRC_END = _SRC_GATES + 3 * D_MODEL

Z_GATES = 0
N_GATES = 3 * D_MODEL
Z_CC = N_GATES
Z_CX = Z_CC + CONV_CH
Z_CB = Z_CX + CONV_CH
ZC = Z_CB + CONV_CH
M_Q, M_K, M_V, M_R = 0, GLA_W, 2 * GLA_W, 3 * GLA_W
M_SQ = 4 * GLA_W
M_SK = M_SQ + SWA_QW
M_SV = M_SK + SWA_KW
ZM = M_SV + SWA_KW
NZ = ZC + ZM

LANE = 128
SUBLANE = 8
TOKEN_TILE = 512
MERGE_TILE = 1024
FF_TILE = 512
PROJ_CHUNK = 512
VMEM_LIMIT = 56 * 1024 * 1024


def _rms(x, g):
    return x * lax.rsqrt(jnp.mean(x * x, axis=-1, keepdims=True) + EPS) * g


def _dot(a, b):
    return jnp.dot(a, b, preferred_element_type=F32)


def _dot_nt(a, b):
    return lax.dot_general(a, b, (((1,), (1,)), ((), ())), preferred_element_type=F32)


def _dot_tn(a, b):
    return lax.dot_general(a, b, (((0,), (0,)), ((), ())), preferred_element_type=F32)


def _log_sigmoid(x):
    return jnp.minimum(x, 0.0) - jnp.log(1.0 + jnp.exp(-jnp.abs(x)))


def _split3(x):
    hi = x.astype(BF16)
    r1 = x - hi.astype(F32)
    mid = r1.astype(BF16)
    lo = (r1 - mid.astype(F32)).astype(BF16)
    return hi, mid, lo


def _resident(shape, layer=None):
    nd = len(shape)
    if layer is None:
        return pl.BlockSpec(shape, lambda *_: (0,) * nd, pipeline_mode=pl.Buffered(1))
    return pl.BlockSpec((None,) + tuple(shape), lambda *_: (layer,) + (0,) * nd, pipeline_mode=pl.Buffered(1))


def _params(n_axes):
    return pltpu.CompilerParams(
        dimension_semantics=("arbitrary",) * n_axes, vmem_limit_bytes=VMEM_LIMIT)


def _causal_conv3(y, prev, w):
    n = y.shape[0]
    ext = jnp.concatenate([prev, y], axis=0)
    y1 = ext[SUBLANE - 1:SUBLANE - 1 + n, :]
    y2 = ext[SUBLANE - 2:SUBLANE - 2 + n, :]
    return w[0:1, :] * y2 + w[1:2, :] * y1 + w[2:3, :] * y


def _gla_stages(seq_start, q_ref, k_ref, v_ref, r_ref, ga_ref, wa3_ref, ba_ref, g_ref, o_ref, st_ref):
    c_len = GLA_CHUNK
    n_chunks = o_ref.shape[0] // c_len

    @pl.when(seq_start)
    def _():
        st_ref[...] = jnp.zeros_like(st_ref)

    row = lax.broadcasted_iota(jnp.int32, (c_len, c_len), 0)
    col = lax.broadcasted_iota(jnp.int32, (c_len, c_len), 1)
    causal = row >= col
    row3 = lax.broadcasted_iota(jnp.int32, (c_len, 3 * c_len), 0)
    col3 = lax.broadcasted_iota(jnp.int32, (c_len, 3 * c_len), 1)
    scale = GLA_DK ** -0.5
    gain = g_ref[...]
    heads = range(GLA_HEADS)
    hsl = [slice(h * GLA_DK, (h + 1) * GLA_DK) for h in heads]
    rsl = [slice(c * c_len, (c + 1) * c_len) for c in range(n_chunks)]

    ga = ga_ref[...]
    ga_hi = ga.astype(BF16).astype(F32)
    ga_lo = ga - ga_hi
    ga3 = (ga_hi + pltpu.roll(ga_lo, GLA_RANK, 1) + pltpu.roll(ga_hi, 2 * GLA_RANK, 1)).astype(BF16)
    pre = _dot(ga3, wa3_ref[...]) + ba_ref[...]
    yield
    la = _log_sigmoid(pre) * (1.0 / GLA_TAU)
    la_c = jnp.concatenate([la[rs, :] for rs in rsl], axis=1)
    yield
    tril3 = jnp.where(row3 >= col3 % c_len, 1.0, 0.0).astype(BF16)
    b_c = _dot(tril3, jnp.concatenate(_split3(la_c), axis=0))
    yield

    def scaled(c):
        b = b_c[:, c * GLA_W:(c + 1) * GLA_W]
        b_mid = b[c_len // 2 - 1:c_len // 2, :]
        b_last = b[c_len - 1:c_len, :]
        e_q = jnp.exp(b - b_mid)
        q_in = q_ref[rsl[c], :].astype(F32) * scale * e_q
        k_in = k_ref[rsl[c], :].astype(F32) / e_q
        return dict(
            q_in=q_in.astype(BF16),
            k_in=k_in.astype(BF16),
            q_ex=(q_in * jnp.exp(b_mid)).astype(BF16),
            k_ex=(k_in * jnp.exp(b_last - b_mid)).astype(BF16),
            decay=jnp.exp(b_last),
            v=v_ref[rsl[c], :])

    def scores(s):
        s["attn"] = [_dot_nt(s["q_in"][:, hs], s["k_in"][:, hs]) for hs in hsl]
        s["kv"] = [_dot_tn(s["v"][:, hs], s["k_ex"][:, hs]) for hs in hsl]

    def masked(s):
        s["attn"] = [jnp.where(causal, a, 0.0).astype(BF16) for a in s["attn"]]

    def intra(s):
        s["o"] = [_dot(a, s["v"][:, hs]) for a, hs in zip(s["attn"], hsl)]

    quarters = [list(range(q * n_chunks // 4, (q + 1) * n_chunks // 4)) for q in range(4)]
    parts = {}
    for qt in quarters:
        for c in qt:
            parts[c] = scaled(c)
        yield
        for c in qt:
            scores(parts[c])
    yield
    for qt in quarters:
        for c in qt:
            masked(parts[c])
        yield
        for c in qt:
            intra(parts[c])
    yield

    states = [st_ref[h] for h in heads]
    for c in range(n_chunks):
        s = parts[c]
        inter = [_dot_nt(s["q_ex"][:, hs], states[h].astype(BF16)) for h, hs in enumerate(hsl)]
        states = [states[h] * s["decay"][:, hs] + s["kv"][h] for h, hs in enumerate(hsl)]
        if c % 2 == 0:
            yield
        for h, hs in enumerate(hsl):
            o = s["o"][h] + inter[h]
            o = o * lax.rsqrt(jnp.mean(o * o, axis=-1, keepdims=True) + EPS) * gain[:, hs]
            hr = 0.5 * r_ref[rsl[c], hs].astype(F32)
            o_ref[rsl[c], hs] = (o * (hr + hr * jnp.tanh(hr))).astype(BF16)
    for h in heads:
        st_ref[h] = states[h]


def _swa_stages(seq_start, sink_ref, q_ref, k_ref, v_ref, kp, vp, o_ref):
    w = SWA_WINDOW
    hd = SWA_HEAD_DIM
    grp = SWA_GROUP
    n_blk = q_ref.shape[0] // w
    scale = hd ** -0.5

    row = lax.broadcasted_iota(jnp.int32, (grp * w, 2 * w), 0)
    col = lax.broadcasted_iota(jnp.int32, (grp * w, 2 * w), 1)
    head_in_group = row // w
    dist = w + (row % w) - col
    valid = jnp.logical_and(dist >= 0, dist < w)
    no_prev = jnp.logical_and(seq_start, col < w)
    grp_col = lax.broadcasted_iota(jnp.int32, (grp * w, 1), 0) // w

    bias, bias_first, sinks, keys, vals = [], [], [], [], []
    for h in range(SWA_KV_HEADS):
        slope = jnp.zeros((grp * w, 2 * w), F32)
        sink = jnp.zeros((grp * w, 1), F32)
        for g in range(grp):
            hq = h * grp + g
            slope = jnp.where(head_in_group == g, 2.0 ** (-(8.0 / SWA_Q_HEADS) * (hq + 1)), slope)
            sink = jnp.where(grp_col == g, sink_ref[hq], sink)
        b = jnp.where(valid, -slope * dist.astype(F32), -jnp.inf)
        bias.append(b)
        bias_first.append(jnp.where(no_prev, -jnp.inf, b))
        sinks.append(sink)
        ks = slice(h * hd, (h + 1) * hd)
        keys.append(jnp.concatenate([kp[:, ks], k_ref[:, ks]], axis=0))
        v_h = jnp.concatenate([vp[:, ks], v_ref[:, ks]], axis=0)
        vals.append(jnp.concatenate([v_h, jnp.ones_like(v_h)], axis=1))
    yield

    def scores(h, j):
        qs = [q_ref[j * w:(j + 1) * w, (h * grp + g) * hd:(h * grp + g + 1) * hd] for g in range(grp)]
        q = (jnp.concatenate(qs, axis=0).astype(F32) * scale).astype(BF16)
        return _dot_nt(q, keys[h][j * w:(j + 2) * w, :])

    def finish(h, j, s):
        logits = s + (bias_first[h] if j == 0 else bias[h])
        m = jnp.maximum(jnp.max(logits, axis=-1, keepdims=True), sinks[h])
        p = jnp.exp(logits - m).astype(BF16)
        ov = _dot(p, vals[h][j * w:(j + 2) * w, :])
        o = ov[:, :hd] / (ov[:, hd:] + jnp.exp(sinks[h] - m))
        for g in range(grp):
            hq = h * grp + g
            o_ref[j * w:(j + 1) * w, hq * hd:(hq + 1) * hd] = o[g * w:(g + 1) * w, :].astype(BF16)

    jobs = [(h, j) for h in range(SWA_KV_HEADS) for j in range(n_blk)]
    s_next = scores(*jobs[0])
    for idx, (h, j) in enumerate(jobs):
        s = s_next
        if idx + 1 < len(jobs):
            s_next = scores(*jobs[idx + 1])
        yield
        finish(h, j, s)


def _front_kernel(tiles_per_seq, n_tiles, layer, h_ref, g_ref, w_ref, wga_ref, wa3_ref, ba_ref,
                  gn_ref, cw_ref, sink_ref, tg_ref, cv_ref, og_ref, os_ref,
                  zm_ref, ga_scr, kvt_ref, st_ref, ptail_ref):
    i = pl.program_id(0)
    cur = i % 2
    prv = 1 - cur
    n = h_ref.shape[0]
    tile = jnp.minimum(i, n_tiles - 1)

    @pl.when(i == 0)
    def _():
        zm_ref[1] = jnp.zeros(zm_ref.shape[1:], zm_ref.dtype)
        ga_scr[1] = jnp.zeros(ga_scr.shape[1:], ga_scr.dtype)
        kvt_ref[...] = jnp.zeros_like(kvt_ref)
        ptail_ref[...] = jnp.zeros_like(ptail_ref)

    u = _rms(h_ref[...], g_ref[...]).astype(BF16)
    seq_start = jnp.maximum(i - 1, 0) % tiles_per_seq == 0
    zprev = lambda off, width: zm_ref.at[prv, :, off:off + width]
    kv_tail = kvt_ref[...]
    mixers = [
        _gla_stages(seq_start, zprev(M_Q, GLA_W), zprev(M_K, GLA_W), zprev(M_V, GLA_W), zprev(M_R, GLA_W),
                    ga_scr.at[prv], wa3_ref, ba_ref, gn_ref, og_ref, st_ref),
        _swa_stages(seq_start, sink_ref.at[layer], zprev(M_SQ, SWA_QW), zprev(M_SK, SWA_KW), zprev(M_SV, SWA_KW),
                    kv_tail[:, :SWA_KW], kv_tail[:, SWA_KW:], os_ref),
    ]

    def advance():
        for gen in list(mixers):
            try:
                next(gen)
            except StopIteration:
                mixers.remove(gen)

    held = {}
    for off in range(0, NZ, PROJ_CHUNK):
        wd = min(PROJ_CHUNK, NZ - off)
        res = _dot(u, w_ref[:, off:off + wd])
        if off < N_GATES:
            tg_ref[:, off:off + wd] = jnp.tanh(0.5 * res).astype(BF16)
        elif off == Z_CC:
            held["cc"] = res
        elif off == Z_CX:
            prod = held.pop("cc") * res
            prev = jnp.where(tile % tiles_per_seq == 0, 0.0, ptail_ref[(tile + 1) % 2])
            ptail_ref[tile % 2] = prod[n - SUBLANE:, :]
            held["conv"] = _causal_conv3(prod, prev, cw_ref[...])
        elif off == Z_CB:
            cv_ref[...] = (res * held.pop("conv")).astype(BF16)
        else:
            zm_ref[cur, :, off - ZC:off - ZC + wd] = res.astype(BF16)
        advance()
    ga_scr[cur] = _dot(u, wga_ref[...])
    while mixers:
        advance()
    kvt_ref[...] = zm_ref[prv, n - SWA_WINDOW:, M_SK:M_SK + 2 * SWA_KW]


def _front(layer, h, g, w_main, w_ga, wa3, b_alpha, gain, conv_w, sinks, seq_len):
    t = h.shape[0]
    tm = TOKEN_TILE
    n_tiles = t // tm
    cur_blk = lambda i: (jnp.minimum(i, n_tiles - 1), 0)
    prev_blk = lambda i: (jnp.maximum(i - 1, 0), 0)
    return pl.pallas_call(
        functools.partial(_front_kernel, seq_len // tm, n_tiles, layer),
        grid=(n_tiles + 1,),
        in_specs=[
            pl.BlockSpec((tm, D_MODEL), cur_blk),
            _resident((1, D_MODEL), layer),
            _resident((D_MODEL, NZ), layer),
            _resident((D_MODEL, LANE), layer),
            _resident((LANE, GLA_W), layer),
            _resident((1, GLA_W), layer), _resident((1, GLA_W), layer),
            _resident((3, CONV_CH), layer),
            pl.BlockSpec(memory_space=pltpu.SMEM),
        ],
        out_specs=[
            pl.BlockSpec((tm, N_GATES), cur_blk),
            pl.BlockSpec((tm, CONV_CH), cur_blk),
            pl.BlockSpec((tm, GLA_W), prev_blk),
            pl.BlockSpec((tm, SWA_QW), prev_blk),
        ],
        out_shape=[
            jax.ShapeDtypeStruct((t, N_GATES), BF16),
            jax.ShapeDtypeStruct((t, CONV_CH), BF16),
            jax.ShapeDtypeStruct((t, GLA_W), BF16),
            jax.ShapeDtypeStruct((t, SWA_QW), BF16),
        ],
        scratch_shapes=[pltpu.VMEM((2, tm, ZM), BF16),
                        pltpu.VMEM((2, tm, LANE), F32),
                        pltpu.VMEM((SWA_WINDOW, 2 * SWA_KW), BF16),
                        pltpu.VMEM((GLA_HEADS, GLA_DV, GLA_DK), F32),
                        pltpu.VMEM((2, SUBLANE, CONV_CH), F32)],
        compiler_params=_params(1),
        name="front",
    )(h, g, w_main, w_ga, wa3, b_alpha, gain, conv_w, sinks)


def _merge_kernel(h_ref, og_ref, os_ref, cv_ref, t1_ref, t2_ref, t3_ref,
                  wg_ref, wc_ref, ws_ref, wo_ref, o_ref):
    y_gla = _dot(og_ref[...], wg_ref[...])
    y_conv = _dot(cv_ref[...], wc_ref[...])
    y_swa = _dot(os_ref[...], ws_ref[...])

    def gate(t_ref, y):
        return y + t_ref[...].astype(F32) * y

    merged = 0.5 * (gate(t1_ref, y_gla) + gate(t2_ref, y_conv) + gate(t3_ref, y_swa))
    o_ref[...] = h_ref[...] + _dot(merged.astype(BF16), wo_ref[...])


def _merge(layer, h, o_gla, o_swa, cv, tg, w_gla_o, w_conv_o, w_swa_o, w_o):
    t = h.shape[0]
    tm = MERGE_TILE
    row = lambda width, j=0: pl.BlockSpec((tm, width), lambda i: (i, j))
    return pl.pallas_call(
        _merge_kernel,
        grid=(t // tm,),
        in_specs=[
            row(D_MODEL), row(GLA_W), row(SWA_QW), row(CONV_CH),
            row(D_MODEL, 0), row(D_MODEL, 1), row(D_MODEL, 2),
            _resident((GLA_W, D_MODEL), layer),
            _resident((CONV_CH, D_MODEL), layer),
            _resident((SWA_QW, D_MODEL), layer),
            _resident((D_MODEL, D_MODEL), layer),
        ],
        out_specs=row(D_MODEL),
        out_shape=jax.ShapeDtypeStruct((t, D_MODEL), F32),
        compiler_params=_params(1),
        name="merge",
    )(h, o_gla, o_swa, cv, tg, tg, tg, w_gla_o, w_conv_o, w_swa_o, w_o)


def _ffn_kernel(tiles_per_seq, final_norm, h_ref, g_ref, wup_ref, cw_ref, wd_ref, gf_ref, o_ref,
                act_ref, tail_ref):
    @pl.when(pl.program_id(0) % tiles_per_seq == 0)
    def _():
        tail_ref[...] = jnp.zeros_like(tail_ref)

    x = h_ref[...]
    u = _rms(x, g_ref[...]).astype(BF16)
    n = x.shape[0]
    nf = -(-D_FF // FF_TILE)
    gcol = lambda c: slice(c * FF_TILE, min((c + 1) * FF_TILE, D_FF))
    vcol = lambda c: slice(D_FF + c * FF_TILE, D_FF + min((c + 1) * FF_TILE, D_FF))

    def up(c):
        return _dot(u, wup_ref[:, gcol(c)]), _dot(u, wup_ref[:, vcol(c)])

    def gated(c, hid):
        hg, hv = hid
        pg = tail_ref[0, :, gcol(c)]
        pv = tail_ref[1, :, gcol(c)]
        tail_ref[0, :, gcol(c)] = hg[n - SUBLANE:, :]
        tail_ref[1, :, gcol(c)] = hv[n - SUBLANE:, :]
        s = _causal_conv3(hg, pg, 0.5 * cw_ref[:, gcol(c)])
        v = _causal_conv3(hv, pv, cw_ref[:, vcol(c)])
        act_ref[:, gcol(c)] = ((s + s * jnp.tanh(s)) * v).astype(BF16)

    hid = up(0)
    for c in range(nf):
        nxt = up(c + 1) if c + 1 < nf else None
        gated(c, hid)
        hid = nxt
    out = x + _dot(act_ref[...], wd_ref[...])
    if final_norm:
        out = _rms(out, gf_ref[...])
    o_ref[...] = out


def _ffn(layer, h, g, wup, cw, wd, g_final, seq_len, final_norm):
    t = h.shape[0]
    tm = TOKEN_TILE
    return pl.pallas_call(
        functools.partial(_ffn_kernel, seq_len // tm, final_norm),
        grid=(t // tm,),
        in_specs=[
            pl.BlockSpec((tm, D_MODEL), lambda i: (i, 0)),
            _resident((1, D_MODEL), layer),
            _resident((D_MODEL, 2 * D_FF), layer),
            _resident((3, 2 * D_FF), layer),
            _resident((D_FF, D_MODEL), layer),
            _resident((1, D_MODEL)),
        ],
        out_specs=pl.BlockSpec((tm, D_MODEL), lambda i: (i, 0)),
        out_shape=jax.ShapeDtypeStruct((t, D_MODEL), F32),
        scratch_shapes=[pltpu.VMEM((tm, D_FF), BF16),
                        pltpu.VMEM((2, SUBLANE, D_FF), F32)],
        compiler_params=_params(1),
        name="ffn",
    )(h, g, wup, cw, wd, g_final)


_W_PIECES = (
    (_SRC_GATES, 3 * D_MODEL),
    (_SRC_CONV + 2 * CONV_CH, CONV_CH),
    (_SRC_CONV, CONV_CH),
    (_SRC_CONV + CONV_CH, CONV_CH),
    (_SRC_GLA, 4 * GLA_W),
    (_SRC_SWA, SWA_QW + 2 * SWA_KW),
)
W_COLS = 256


def _relayout_kernel(wt_ref, wm_ref, wga_ref):
    off = 0
    for src, width in _W_PIECES:
        wm_ref[:, off:off + width] = wt_ref[src:src + width, :].T.astype(BF16)
        off += width
    lane = lax.broadcasted_iota(jnp.int32, wga_ref.shape, 1)
    wga_ref[...] = jnp.where(lane < GLA_RANK, wt_ref[_SRC_GA:_SRC_GA + LANE, :].T, 0.0).astype(BF16)


def _relayout(w_t):
    depth, n, k = w_t.shape
    return pl.pallas_call(
        _relayout_kernel,
        grid=(depth, k // W_COLS),
        in_specs=[pl.BlockSpec((None, n, W_COLS), lambda l, i: (l, 0, i))],
        out_specs=[pl.BlockSpec((None, W_COLS, NZ), lambda l, i: (l, i, 0)),
                   pl.BlockSpec((None, W_COLS, LANE), lambda l, i: (l, i, 0))],
        out_shape=[jax.ShapeDtypeStruct((depth, k, NZ), BF16), jax.ShapeDtypeStruct((depth, k, LANE), BF16)],
        compiler_params=_params(2),
        name="relayout",
    )(w_t)


def _prep_weights(w_in, w_alpha):
    w_main, w_ga = _relayout(jnp.swapaxes(w_in, 1, 2))
    wa_hi = w_alpha.astype(BF16)
    wa_lo = (w_alpha - wa_hi.astype(F32)).astype(BF16)
    wa3 = jnp.pad(jnp.concatenate([wa_hi, wa_hi, wa_lo], axis=1), ((0, 0), (0, LANE - 3 * GLA_RANK), (0, 0)))
    return w_main, w_ga, wa3


def kernel(x, g_mix, w_in, gla_w_alpha, gla_b_alpha, gla_norm_g, conv_w, swa_sinks, w_gla_o,
           w_conv_o, w_swa_o, w_o, g_ffn, w_up, ffn_conv_w, w_down, g_final):
    bsz, seq_len, d = x.shape
    assert d == D_MODEL and seq_len % TOKEN_TILE == 0 and (bsz * seq_len) % MERGE_TILE == 0
    depth = w_in.shape[0]
    w_main, w_ga, wa3 = _prep_weights(w_in, gla_w_alpha)
    row = lambda p: p[:, None, :]
    g_mix, b_alpha, gain, g_ffn = row(g_mix), row(gla_b_alpha), row(gla_norm_g), row(g_ffn)
    w_gla_o, w_conv_o, w_swa_o, w_o = (w.astype(BF16) for w in (w_gla_o, w_conv_o, w_swa_o, w_o))
    w_up, w_down = w_up.astype(BF16), w_down.astype(BF16)
    h = x.reshape(bsz * seq_len, d)
    for l in range(depth):
        tg, cv, o_gla, o_swa = _front(l, h, g_mix, w_main, w_ga, wa3, b_alpha, gain, conv_w,
                                      swa_sinks, seq_len)
        h = _merge(l, h, o_gla, o_swa, cv, tg, w_gla_o, w_conv_o, w_swa_o, w_o)
        h = _ffn(l, h, g_ffn, w_up, ffn_conv_w, w_down, g_final[None, :], seq_len, final_norm=(l == depth - 1))
    return h.reshape(bsz, seq_len, d)
```

```python
import functools

import jax
import jax.numpy as jnp
from jax import lax
from jax.experimental import pallas as pl
from jax.experimental.pallas import tpu as pltpu

F32 = jnp.float32
BF16 = jnp.bfloat16

D_MODEL = 1024
GLA_HEADS = 4
GLA_DK = 128
GLA_DV = 128
GLA_RANK = 16
GLA_TAU = 16.0
GLA_CHUNK = 64
CONV_CH = D_MODEL // 2
SWA_Q_HEADS = 8
SWA_KV_HEADS = 2
SWA_GROUP = SWA_Q_HEADS // SWA_KV_HEADS
SWA_HEAD_DIM = 64
SWA_WINDOW = 128
D_FF = 2816
EPS = 1e-6

GLA_W = GLA_HEADS * GLA_DK
SWA_QW = SWA_Q_HEADS * SWA_HEAD_DIM
SWA_KW = SWA_KV_HEADS * SWA_HEAD_DIM

_SRC_GLA = 0
_SRC_GA = 4 * GLA_W
_SRC_CONV = _SRC_GA + GLA_RANK
_SRC_SWA = _SRC_CONV + 3 * CONV_CH
_SRC_GATES = _SRC_SWA + SWA_QW + 2 * SWA_KW
_SRC_END = _SRC_GATES + 3 * D_MODEL

Z_GATES = 0
N_GATES = 3 * D_MODEL
Z_CC = N_GATES
Z_CX = Z_CC + CONV_CH
Z_CB = Z_CX + CONV_CH
ZC = Z_CB + CONV_CH
M_Q, M_K, M_V, M_R = 0, GLA_W, 2 * GLA_W, 3 * GLA_W
M_SQ = 4 * GLA_W
M_SK = M_SQ + SWA_QW
M_SV = M_SK + SWA_KW
ZM = M_SV + SWA_KW
NZ = ZC + ZM

LANE = 128
SUBLANE = 8
TOKEN_TILE = 512
MERGE_TILE = 1024
FF_TILE = 512
PROJ_CHUNK = 512
VMEM_LIMIT = 56 * 1024 * 1024


def _rms(x, g):
    return x * lax.rsqrt(jnp.mean(x * x, axis=-1, keepdims=True) + EPS) * g


def _dot(a, b):
    return jnp.dot(a, b, preferred_element_type=F32)


def _dot_nt(a, b):
    return lax.dot_general(a, b, (((1,), (1,)), ((), ())), preferred_element_type=F32)


def _dot_tn(a, b):
    return lax.dot_general(a, b, (((0,), (0,)), ((), ())), preferred_element_type=F32)


def _log_sigmoid(x):
    return jnp.minimum(x, 0.0) - jnp.log(1.0 + jnp.exp(-jnp.abs(x)))


def _split3(x):
    hi = x.astype(BF16)
    r1 = x - hi.astype(F32)
    mid = r1.astype(BF16)
    lo = (r1 - mid.astype(F32)).astype(BF16)
    return hi, mid, lo


def _resident(shape, layer=None):
    nd = len(shape)
    if layer is None:
        return pl.BlockSpec(shape, lambda *_: (0,) * nd, pipeline_mode=pl.Buffered(1))
    return pl.BlockSpec((None,) + tuple(shape), lambda *_: (layer,) + (0,) * nd, pipeline_mode=pl.Buffered(1))


def _params(n_axes):
    return pltpu.CompilerParams(
        dimension_semantics=("arbitrary",) * n_axes, vmem_limit_bytes=VMEM_LIMIT)


def _causal_conv3(y, prev, w):
    n = y.shape[0]
    ext = jnp.concatenate([prev, y], axis=0)
    y1 = ext[SUBLANE - 1:SUBLANE - 1 + n, :]
    y2 = ext[SUBLANE - 2:SUBLANE - 2 + n, :]
    return w[0:1, :] * y2 + w[1:2, :] * y1 + w[2:3, :] * y


def _gla_stages(seq_start, q_ref, k_ref, v_ref, r_ref, ga_ref, wa3_ref, ba_ref, g_ref, o_ref, st_ref):
    c_len = GLA_CHUNK
    n_chunks = o_ref.shape[0] // c_len

    @pl.when(seq_start)
    def _():
        st_ref[...] = jnp.zeros_like(st_ref)

    row = lax.broadcasted_iota(jnp.int32, (c_len, c_len), 0)
    col = lax.broadcasted_iota(jnp.int32, (c_len, c_len), 1)
    causal = row >= col
    row3 = lax.broadcasted_iota(jnp.int32, (c_len, 3 * c_len), 0)
    col3 = lax.broadcasted_iota(jnp.int32, (c_len, 3 * c_len), 1)
    scale = GLA_DK ** -0.5
    gain = g_ref[...]
    heads = range(GLA_HEADS)
    hsl = [slice(h * GLA_DK, (h + 1) * GLA_DK) for h in heads]
    rsl = [slice(c * c_len, (c + 1) * c_len) for c in range(n_chunks)]

    ga = ga_ref[...]
    ga_hi = ga.astype(BF16).astype(F32)
    ga_lo = ga - ga_hi
    ga3 = (ga_hi + pltpu.roll(ga_lo, GLA_RANK, 1) + pltpu.roll(ga_hi, 2 * GLA_RANK, 1)).astype(BF16)
    pre = _dot(ga3, wa3_ref[...]) + ba_ref[...]
    yield
    la = _log_sigmoid(pre) * (1.0 / GLA_TAU)
    la_c = jnp.concatenate([la[rs, :] for rs in rsl], axis=1)
    yield
    tril3 = jnp.where(row3 >= col3 % c_len, 1.0, 0.0).astype(BF16)
    b_c = _dot(tril3, jnp.concatenate(_split3(la_c), axis=0))
    yield

    def scaled(c):
        b = b_c[:, c * GLA_W:(c + 1) * GLA_W]
        b_mid = b[c_len // 2 - 1:c_len // 2, :]
        b_last = b[c_len - 1:c_len, :]
        e_q = jnp.exp(b - b_mid)
        q_in = q_ref[rsl[c], :].astype(F32) * scale * e_q
        k_in = k_ref[rsl[c], :].astype(F32) / e_q
        return dict(
            q_in=q_in.astype(BF16),
            k_in=k_in.astype(BF16),
            q_ex=(q_in * jnp.exp(b_mid)).astype(BF16),
            k_ex=(k_in * jnp.exp(b_last - b_mid)).astype(BF16),
            decay_col=[jnp.exp(jnp.concatenate([b[:, hs], b[:, hs]], axis=0).T[:, c_len - 1:c_len])
                       for hs in hsl],
            v=v_ref[rsl[c], :])

    def scores(s):
        s["attn"] = [_dot_nt(s["q_in"][:, hs], s["k_in"][:, hs]) for hs in hsl]
        s["kv"] = [_dot_tn(s["k_ex"][:, hs], s["v"][:, hs]) for hs in hsl]

    def masked(s):
        s["attn"] = [jnp.where(causal, a, 0.0).astype(BF16) for a in s["attn"]]

    quarters = [list(range(q * n_chunks // 4, (q + 1) * n_chunks // 4)) for q in range(4)]
    parts = {}
    for qt in quarters:
        for c in qt:
            parts[c] = scaled(c)
        yield
        for c in qt:
            scores(parts[c])
    yield
    for qt in quarters:
        for c in qt:
            masked(parts[c])
        yield

    states = [st_ref[h] for h in heads]
    for c in range(n_chunks):
        s = parts[c]
        outs = [_dot(jnp.concatenate([s["q_ex"][:, hs], s["attn"][h]], axis=1),
                     jnp.concatenate([states[h].astype(BF16), s["v"][:, hs]], axis=0))
                for h, hs in enumerate(hsl)]
        states = [states[h] * s["decay_col"][h] + s["kv"][h] for h in heads]
        if c % 2 == 0:
            yield
        for h, hs in enumerate(hsl):
            o = outs[h]
            o = o * lax.rsqrt(jnp.mean(o * o, axis=-1, keepdims=True) + EPS) * gain[:, hs]
            hr = 0.5 * r_ref[rsl[c], hs].astype(F32)
            o_ref[rsl[c], hs] = (o * (hr + hr * jnp.tanh(hr))).astype(BF16)
    for h in heads:
        st_ref[h] = states[h]


def _swa_stages(seq_start, sink_ref, q_ref, k_ref, v_ref, kp, vp, o_ref):
    w = SWA_WINDOW
    hd = SWA_HEAD_DIM
    grp = SWA_GROUP
    n_blk = q_ref.shape[0] // w
    scale = hd ** -0.5

    row = lax.broadcasted_iota(jnp.int32, (grp * w, 2 * w), 0)
    col = lax.broadcasted_iota(jnp.int32, (grp * w, 2 * w), 1)
    head_in_group = row // w
    dist = w + (row % w) - col
    valid = jnp.logical_and(dist >= 0, dist < w)
    no_prev = jnp.logical_and(seq_start, col < w)
    grp_col = lax.broadcasted_iota(jnp.int32, (grp * w, 1), 0) // w

    bias, bias_first, sinks, keys, vals = [], [], [], [], []
    for h in range(SWA_KV_HEADS):
        slope = jnp.zeros((grp * w, 2 * w), F32)
        sink = jnp.zeros((grp * w, 1), F32)
        for g in range(grp):
            hq = h * grp + g
            slope = jnp.where(head_in_group == g, 2.0 ** (-(8.0 / SWA_Q_HEADS) * (hq + 1)), slope)
            sink = jnp.where(grp_col == g, sink_ref[hq], sink)
        b = jnp.where(valid, -slope * dist.astype(F32), -jnp.inf)
        bias.append(b)
        bias_first.append(jnp.where(no_prev, -jnp.inf, b))
        sinks.append(sink)
        ks = slice(h * hd, (h + 1) * hd)
        keys.append(jnp.concatenate([kp[:, ks], k_ref[:, ks]], axis=0))
        v_h = jnp.concatenate([vp[:, ks], v_ref[:, ks]], axis=0)
        vals.append(jnp.concatenate([v_h, jnp.ones_like(v_h)], axis=1))
    yield

    def scores(h, j):
        qs = [q_ref[j * w:(j + 1) * w, (h * grp + g) * hd:(h * grp + g + 1) * hd] for g in range(grp)]
        q = (jnp.concatenate(qs, axis=0).astype(F32) * scale).astype(BF16)
        return _dot_nt(q, keys[h][j * w:(j + 2) * w, :])

    def finish(h, j, s):
        logits = s + (bias_first[h] if j == 0 else bias[h])
        m = jnp.maximum(jnp.max(logits, axis=-1, keepdims=True), sinks[h])
        p = jnp.exp(logits - m).astype(BF16)
        ov = _dot(p, vals[h][j * w:(j + 2) * w, :])
        o = ov[:, :hd] / (ov[:, hd:] + jnp.exp(sinks[h] - m))
        for g in range(grp):
            hq = h * grp + g
            o_ref[j * w:(j + 1) * w, hq * hd:(hq + 1) * hd] = o[g * w:(g + 1) * w, :].astype(BF16)

    jobs = [(h, j) for h in range(SWA_KV_HEADS) for j in range(n_blk)]
    s_next = scores(*jobs[0])
    for idx, (h, j) in enumerate(jobs):
        s = s_next
        if idx + 1 < len(jobs):
            s_next = scores(*jobs[idx + 1])
        yield
        finish(h, j, s)


def _front_kernel(tiles_per_seq, n_tiles, layer, h_ref, g_ref, w_ref, wga_ref, wa3_ref, ba_ref,
                  gn_ref, cw_ref, sink_ref, tg_ref, cv_ref, og_ref, os_ref,
                  zm_ref, ga_scr, kvt_ref, st_ref, ptail_ref):
    i = pl.program_id(0)
    cur = i % 2
    prv = 1 - cur
    n = h_ref.shape[0]
    tile = jnp.minimum(i, n_tiles - 1)

    @pl.when(i == 0)
    def _():
        zm_ref[1] = jnp.zeros(zm_ref.shape[1:], zm_ref.dtype)
        ga_scr[1] = jnp.zeros(ga_scr.shape[1:], ga_scr.dtype)
        kvt_ref[...] = jnp.zeros_like(kvt_ref)
        ptail_ref[...] = jnp.zeros_like(ptail_ref)

    u = _rms(h_ref[...], g_ref[...]).astype(BF16)
    seq_start = jnp.maximum(i - 1, 0) % tiles_per_seq == 0
    zprev = lambda off, width: zm_ref.at[prv, :, off:off + width]
    kv_tail = kvt_ref[...]
    mixers = [
        _gla_stages(seq_start, zprev(M_Q, GLA_W), zprev(M_K, GLA_W), zprev(M_V, GLA_W), zprev(M_R, GLA_W),
                    ga_scr.at[prv], wa3_ref, ba_ref, gn_ref, og_ref, st_ref),
        _swa_stages(seq_start, sink_ref.at[layer], zprev(M_SQ, SWA_QW), zprev(M_SK, SWA_KW), zprev(M_SV, SWA_KW),
                    kv_tail[:, :SWA_KW], kv_tail[:, SWA_KW:], os_ref),
    ]

    def advance():
        for gen in list(mixers):
            try:
                next(gen)
            except StopIteration:
                mixers.remove(gen)

    held = {}
    for off in range(0, NZ, PROJ_CHUNK):
        wd = min(PROJ_CHUNK, NZ - off)
        res = _dot(u, w_ref[:, off:off + wd])
        if off < N_GATES:
            tg_ref[:, off:off + wd] = jnp.tanh(0.5 * res).astype(BF16)
        elif off == Z_CC:
            held["cc"] = res
        elif off == Z_CX:
            prod = held.pop("cc") * res
            prev = jnp.where(tile % tiles_per_seq == 0, 0.0, ptail_ref[(tile + 1) % 2])
            ptail_ref[tile % 2] = prod[n - SUBLANE:, :]
            held["conv"] = _causal_conv3(prod, prev, cw_ref[...])
        elif off == Z_CB:
            cv_ref[...] = (res * held.pop("conv")).astype(BF16)
        else:
            zm_ref[cur, :, off - ZC:off - ZC + wd] = res.astype(BF16)
        advance()
    ga_scr[cur] = _dot(u, wga_ref[...])
    while mixers:
        advance()
    kvt_ref[...] = zm_ref[prv, n - SWA_WINDOW:, M_SK:M_SK + 2 * SWA_KW]


def _front(layer, h, g, w_main, w_ga, wa3, b_alpha, gain, conv_w, sinks, seq_len):
    t = h.shape[0]
    tm = TOKEN_TILE
    n_tiles = t // tm
    cur_blk = lambda i: (jnp.minimum(i, n_tiles - 1), 0)
    prev_blk = lambda i: (jnp.maximum(i - 1, 0), 0)
    return pl.pallas_call(
        functools.partial(_front_kernel, seq_len // tm, n_tiles, layer),
        grid=(n_tiles + 1,),
        in_specs=[
            pl.BlockSpec((tm, D_MODEL), cur_blk),
            _resident((1, D_MODEL), layer),
            _resident((D_MODEL, NZ), layer),
            _resident((D_MODEL, LANE), layer),
            _resident((LANE, GLA_W), layer),
            _resident((1, GLA_W), layer), _resident((1, GLA_W), layer),
            _resident((3, CONV_CH), layer),
            pl.BlockSpec(memory_space=pltpu.SMEM),
        ],
        out_specs=[
            pl.BlockSpec((tm, N_GATES), cur_blk),
            pl.BlockSpec((tm, CONV_CH), cur_blk),
            pl.BlockSpec((tm, GLA_W), prev_blk),
            pl.BlockSpec((tm, SWA_QW), prev_blk),
        ],
        out_shape=[
            jax.ShapeDtypeStruct((t, N_GATES), BF16),
            jax.ShapeDtypeStruct((t, CONV_CH), BF16),
            jax.ShapeDtypeStruct((t, GLA_W), BF16),
            jax.ShapeDtypeStruct((t, SWA_QW), BF16),
        ],
        scratch_shapes=[pltpu.VMEM((2, tm, ZM), BF16),
                        pltpu.VMEM((2, tm, LANE), F32),
                        pltpu.VMEM((SWA_WINDOW, 2 * SWA_KW), BF16),
                        pltpu.VMEM((GLA_HEADS, GLA_DV, GLA_DK), F32),
                        pltpu.VMEM((2, SUBLANE, CONV_CH), F32)],
        compiler_params=_params(1),
        name="front",
    )(h, g, w_main, w_ga, wa3, b_alpha, gain, conv_w, sinks)


def _merge_kernel(h_ref, og_ref, os_ref, cv_ref, t1_ref, t2_ref, t3_ref,
                  wg_ref, wc_ref, ws_ref, wo_ref, o_ref):
    y_gla = _dot(og_ref[...], wg_ref[...])
    y_conv = _dot(cv_ref[...], wc_ref[...])
    y_swa = _dot(os_ref[...], ws_ref[...])

    def gate(t_ref, y):
        return y + t_ref[...].astype(F32) * y

    merged = 0.5 * (gate(t1_ref, y_gla) + gate(t2_ref, y_conv) + gate(t3_ref, y_swa))
    o_ref[...] = h_ref[...] + _dot(merged.astype(BF16), wo_ref[...])


def _merge(layer, h, o_gla, o_swa, cv, tg, w_gla_o, w_conv_o, w_swa_o, w_o):
    t = h.shape[0]
    tm = MERGE_TILE
    row = lambda width, j=0: pl.BlockSpec((tm, width), lambda i: (i, j))
    return pl.pallas_call(
        _merge_kernel,
        grid=(t // tm,),
        in_specs=[
            row(D_MODEL), row(GLA_W), row(SWA_QW), row(CONV_CH),
            row(D_MODEL, 0), row(D_MODEL, 1), row(D_MODEL, 2),
            _resident((GLA_W, D_MODEL), layer),
            _resident((CONV_CH, D_MODEL), layer),
            _resident((SWA_QW, D_MODEL), layer),
            _resident((D_MODEL, D_MODEL), layer),
        ],
        out_specs=row(D_MODEL),
        out_shape=jax.ShapeDtypeStruct((t, D_MODEL), F32),
        compiler_params=_params(1),
        name="merge",
    )(h, o_gla, o_swa, cv, tg, tg, tg, w_gla_o, w_conv_o, w_swa_o, w_o)


def _ffn_kernel(tiles_per_seq, final_norm, h_ref, g_ref, wup_ref, cw_ref, wd_ref, gf_ref, o_ref,
                act_ref, tail_ref):
    @pl.when(pl.program_id(0) % tiles_per_seq == 0)
    def _():
        tail_ref[...] = jnp.zeros_like(tail_ref)

    x = h_ref[...]
    u = _rms(x, g_ref[...]).astype(BF16)
    n = x.shape[0]
    nf = -(-D_FF // FF_TILE)
    gcol = lambda c: slice(c * FF_TILE, min((c + 1) * FF_TILE, D_FF))
    vcol = lambda c: slice(D_FF + c * FF_TILE, D_FF + min((c + 1) * FF_TILE, D_FF))

    def up(c):
        return _dot(u, wup_ref[:, gcol(c)]), _dot(u, wup_ref[:, vcol(c)])

    def gated(c, hid):
        hg, hv = hid
        pg = tail_ref[0, :, gcol(c)]
        pv = tail_ref[1, :, gcol(c)]
        tail_ref[0, :, gcol(c)] = hg[n - SUBLANE:, :]
        tail_ref[1, :, gcol(c)] = hv[n - SUBLANE:, :]
        s = _causal_conv3(hg, pg, 0.5 * cw_ref[:, gcol(c)])
        v = _causal_conv3(hv, pv, cw_ref[:, vcol(c)])
        act_ref[:, gcol(c)] = ((s + s * jnp.tanh(s)) * v).astype(BF16)

    hid = up(0)
    for c in range(nf):
        nxt = up(c + 1) if c + 1 < nf else None
        gated(c, hid)
        hid = nxt
    out = x + _dot(act_ref[...], wd_ref[...])
    if final_norm:
        out = _rms(out, gf_ref[...])
    o_ref[...] = out


def _ffn(layer, h, g, wup, cw, wd, g_final, seq_len, final_norm):
    t = h.shape[0]
    tm = TOKEN_TILE
    return pl.pallas_call(
        functools.partial(_ffn_kernel, seq_len // tm, final_norm),
        grid=(t // tm,),
        in_specs=[
            pl.BlockSpec((tm, D_MODEL), lambda i: (i, 0)),
            _resident((1, D_MODEL), layer),
            _resident((D_MODEL, 2 * D_FF), layer),
            _resident((3, 2 * D_FF), layer),
            _resident((D_FF, D_MODEL), layer),
            _resident((1, D_MODEL)),
        ],
        out_specs=pl.BlockSpec((tm, D_MODEL), lambda i: (i, 0)),
        out_shape=jax.ShapeDtypeStruct((t, D_MODEL), F32),
        scratch_shapes=[pltpu.VMEM((tm, D_FF), BF16),
                        pltpu.VMEM((2, SUBLANE, D_FF), F32)],
        compiler_params=_params(1),
        name="ffn",
    )(h, g, wup, cw, wd, g_final)


_W_PIECES = (
    (_SRC_GATES, 3 * D_MODEL),
    (_SRC_CONV + 2 * CONV_CH, CONV_CH),
    (_SRC_CONV, CONV_CH),
    (_SRC_CONV + CONV_CH, CONV_CH),
    (_SRC_GLA, 4 * GLA_W),
    (_SRC_SWA, SWA_QW + 2 * SWA_KW),
)
W_COLS = 256


def _relayout_kernel(wt_ref, wm_ref, wga_ref):
    off = 0
    for src, width in _W_PIECES:
        wm_ref[:, off:off + width] = wt_ref[src:src + width, :].T.astype(BF16)
        off += width
    lane = lax.broadcasted_iota(jnp.int32, wga_ref.shape, 1)
    wga_ref[...] = jnp.where(lane < GLA_RANK, wt_ref[_SRC_GA:_SRC_GA + LANE, :].T, 0.0).astype(BF16)


def _relayout(w_t):
    depth, n, k = w_t.shape
    return pl.pallas_call(
        _relayout_kernel,
        grid=(depth, k // W_COLS),
        in_specs=[pl.BlockSpec((None, n, W_COLS), lambda l, i: (l, 0, i))],
        out_specs=[pl.BlockSpec((None, W_COLS, NZ), lambda l, i: (l, i, 0)),
                   pl.BlockSpec((None, W_COLS, LANE), lambda l, i: (l, i, 0))],
        out_shape=[jax.ShapeDtypeStruct((depth, k, NZ), BF16), jax.ShapeDtypeStruct((depth, k, LANE), BF16)],
        compiler_params=_params(2),
        name="relayout",
    )(w_t)


def _prep_weights(w_in, w_alpha):
    w_main, w_ga = _relayout(jnp.swapaxes(w_in, 1, 2))
    wa_hi = w_alpha.astype(BF16)
    wa_lo = (w_alpha - wa_hi.astype(F32)).astype(BF16)
    wa3 = jnp.pad(jnp.concatenate([wa_hi, wa_hi, wa_lo], axis=1), ((0, 0), (0, LANE - 3 * GLA_RANK), (0, 0)))
    return w_main, w_ga, wa3


def kernel(x, g_mix, w_in, gla_w_alpha, gla_b_alpha, gla_norm_g, conv_w, swa_sinks, w_gla_o,
           w_conv_o, w_swa_o, w_o, g_ffn, w_up, ffn_conv_w, w_down, g_final):
    bsz, seq_len, d = x.shape
    assert d == D_MODEL and seq_len % TOKEN_TILE == 0 and (bsz * seq_len) % MERGE_TILE == 0
    depth = w_in.shape[0]
    w_main, w_ga, wa3 = _prep_weights(w_in, gla_w_alpha)
    row = lambda p: p[:, None, :]
    g_mix, b_alpha, gain, g_ffn = row(g_mix), row(gla_b_alpha), row(gla_norm_g), row(g_ffn)
    w_gla_o, w_conv_o, w_swa_o, w_o = (w.astype(BF16) for w in (w_gla_o, w_conv_o, w_swa_o, w_o))
    w_up, w_down = w_up.astype(BF16), w_down.astype(BF16)
    h = x.reshape(bsz * seq_len, d)
    for l in range(depth):
        tg, cv, o_gla, o_swa = _front(l, h, g_mix, w_main, w_ga, wa3, b_alpha, gain, conv_w,
                                      swa_sinks, seq_len)
        h = _merge(l, h, o_gla, o_swa, cv, tg, w_gla_o, w_conv_o, w_swa_o, w_o)
        h = _ffn(l, h, g_ffn, w_up, ffn_conv_w, w_down, g_final[None, :], seq_len, final_norm=(l == depth - 1))
    return h.reshape(bsz, seq_len, d)
```
